```python
import math
import jax, jax.numpy as jnp
from jax import lax
import numpy as np

D_MODEL = 1024
BATCH = 8
SEQ = 4096
DEPTH = 4

N_MIXERS = 3
NORM_EPS = 1e-6

RW_HEAD = 64
RW_HEADS = D_MODEL // RW_HEAD
RW_WIDTH = RW_HEADS * RW_HEAD
RW_DECAY_RANK = 64
RW_AAA_RANK = 64
RW_VRES_RANK = 32
RW_GN_EPS = 64e-5

SSD_EXPAND = 2
SSD_WIDTH = SSD_EXPAND * D_MODEL
SSD_HEAD = 64
SSD_HEADS = SSD_WIDTH // SSD_HEAD
SSD_GROUPS = 8
SSD_STATE = 128
SSD_CONV = 4
SSD_CHUNK = 128
SSD_CONV_DIM = SSD_WIDTH + 2 * SSD_GROUPS * SSD_STATE

SB_HEAD = 64
SB_HEADS = D_MODEL // SB_HEAD
SB_WIDTH = SB_HEADS * SB_HEAD
SB_BLOCK = 128
SB_SCALE = 1.0 / math.sqrt(SB_HEAD)

kernel_name = "hybrid_rwkv7_ssd_stickbreak_trunk"


def rms_norm(u, g, eps=NORM_EPS):
    u32 = u.astype(jnp.float32)
    y = u32 * lax.rsqrt(jnp.mean(u32 * u32, axis=-1, keepdims=True) + eps)
    return (y * g.astype(jnp.float32)).astype(u.dtype)


def token_shift(h):
    return jnp.pad(h, ((0, 0), (1, 0), (0, 0)))[:, :-1]


def rwkv7_mixer(h, p, v_first):
    mu, w_in, w_up, w0, a_up, a0, k_k, k_a, r_k, gn_w, gn_b, w_out = p[:12]
    Bsz, T, _ = h.shape
    W, H, N = RW_WIDTH, RW_HEADS, RW_HEAD
    dx = token_shift(h) - h
    x_r, x_w, x_k, x_v, x_a, x_g = (h + dx * mu[i] for i in range(6))
    c = np.cumsum([0, W, W, W, W, RW_DECAY_RANK, RW_AAA_RANK])
    r = x_r @ w_in[:, c[0]:c[1]]
    k = x_k @ w_in[:, c[1]:c[2]]
    v = x_v @ w_in[:, c[2]:c[3]]
    g = jax.nn.silu(x_g @ w_in[:, c[3]:c[4]])
    w_lo = jnp.tanh(x_w @ w_in[:, c[4]:c[5]])
    a_lo = x_a @ w_in[:, c[5]:c[6]]
    f32 = jnp.float32
    w_raw = (w0 + w_lo @ w_up).astype(f32)
    log_decay = -jnp.exp(-jax.nn.softplus(-w_raw) - 0.5)
    a = jax.nn.sigmoid((a0 + a_lo @ a_up).astype(f32))
    if v_first is None:
        v_first = v
    else:
        v_up, v0 = p[12:]
        v_lo = x_v @ w_in[:, c[6]:]
        v = v + (v_first - v) * jax.nn.sigmoid(v0 + v_lo @ v_up)

    def heads(t):
        return t.astype(f32).reshape(Bsz, T, H, N)

    kk = heads(k * k_k)
    kk = kk / jnp.maximum(jnp.sqrt(jnp.sum(kk * kk, axis=-1, keepdims=True)), 1e-12)
    k = k.astype(f32) * (1.0 + (a - 1.0) * k_a.astype(f32))
    r_h, k_h, v_h, a_h = heads(r), heads(k), heads(v), heads(a)
    decay_h = jnp.exp(heads(log_decay))

    def step(S, inp):
        r_t, w_t, k_t, v_t, kk_t, a_t = inp
        sa = jnp.einsum('bhvk,bhk->bhv', S, -kk_t)
        S = (S * w_t[:, :, None, :]
             + sa[..., None] * (kk_t * a_t)[:, :, None, :]
             + v_t[..., None] * k_t[:, :, None, :])
        y_t = jnp.einsum('bhvk,bhk->bhv', S, r_t)
        return S, y_t

    tm = lambda t: jnp.moveaxis(t, 1, 0)
    S0 = jnp.zeros((Bsz, H, N, N), f32)
    _, y = lax.scan(step, S0, (tm(r_h), tm(decay_h), tm(k_h), tm(v_h), tm(kk), tm(a_h)))
    y = jnp.moveaxis(y, 0, 1)
    mean = jnp.mean(y, axis=-1, keepdims=True)
    var = jnp.mean(jnp.square(y - mean), axis=-1, keepdims=True)
    y = (y - mean) * lax.rsqrt(var + RW_GN_EPS)
    y = y * gn_w.astype(f32).reshape(H, N) + gn_b.astype(f32).reshape(H, N)
    y = y + jnp.sum(r_h * k_h * r_k.astype(f32), axis=-1, keepdims=True) * v_h
    y = y.reshape(Bsz, T, W).astype(h.dtype) * g
    return y @ w_out, v_first


def causal_depthwise_conv(u, w, b):
    out = lax.conv_general_dilated(
        u, w[:, None, :].astype(u.dtype), window_strides=(1,),
        padding=[(w.shape[0] - 1, 0)], dimension_numbers=('NWC', 'WIO', 'NWC'),
        feature_group_count=u.shape[-1])
    return out + b


def ssd_chunked(x, a, Bm, Cm):
    Bsz, T, H, P = x.shape
    G, R, l = SSD_GROUPS, H // SSD_GROUPS, SSD_CHUNK
    nc = T // l
    x = x.reshape(Bsz, nc, l, G, R, P)
    Bm = Bm.reshape(Bsz, nc, l, G, SSD_STATE)
    Cm = Cm.reshape(Bsz, nc, l, G, SSD_STATE)
    ac = jnp.cumsum(jnp.moveaxis(a.reshape(Bsz, nc, l, G, R), 2, -1), axis=-1)
    causal = jnp.tril(jnp.ones((l, l), bool))
    seg = ac[..., :, None] - ac[..., None, :]
    decay_in = jnp.exp(jnp.where(causal, seg, -jnp.inf))
    cb = jnp.einsum('bclgn,bcsgn->bcgls', Cm, Bm)
    y_diag = jnp.einsum('bcgrls,bcsgrp->bclgrp', cb[:, :, :, None] * decay_in, x)
    decay_states = jnp.moveaxis(jnp.exp(ac[..., -1:] - ac), -1, 2)
    states = jnp.einsum('bclgn,bclgrp->bcgrpn', Bm, x * decay_states[..., None])
    chunk_decay = ac[..., -1]

    def chunk_step(h_prev, inp):
        st, dec = inp
        return h_prev * jnp.exp(dec)[..., None, None] + st, h_prev

    h0 = jnp.zeros((Bsz, G, R, P, SSD_STATE), x.dtype)
    _, h_in = lax.scan(chunk_step, h0, (jnp.moveaxis(states, 1, 0), jnp.moveaxis(chunk_decay, 1, 0)))
    h_in = jnp.moveaxis(h_in, 0, 1)
    decay_out = jnp.moveaxis(jnp.exp(ac), -1, 2)
    y_off = jnp.einsum('bclgn,bcgrpn->bclgrp', Cm, h_in) * decay_out[..., None]
    return (y_diag + y_off).reshape(Bsz, T, H, P)


def mamba2_mixer(h, p):
    w_in, conv_w, conv_b, dt_bias, a_log, d_skip, gnorm_w, w_out = p
    Bsz, T, _ = h.shape
    f32 = jnp.float32
    proj = h @ w_in
    z = proj[..., :SSD_WIDTH]
    xbc = proj[..., SSD_WIDTH:SSD_WIDTH + SSD_CONV_DIM]
    dt_raw = proj[..., SSD_WIDTH + SSD_CONV_DIM:]
    xbc = jax.nn.silu(causal_depthwise_conv(xbc, conv_w, conv_b)).astype(f32)
    gn = SSD_GROUPS * SSD_STATE
    xs = xbc[..., :SSD_WIDTH].reshape(Bsz, T, SSD_HEADS, SSD_HEAD)
    Bm = xbc[..., SSD_WIDTH:SSD_WIDTH + gn].reshape(Bsz, T, SSD_GROUPS, SSD_STATE)
    Cm = xbc[..., SSD_WIDTH + gn:].reshape(Bsz, T, SSD_GROUPS, SSD_STATE)
    dt = jax.nn.softplus((dt_raw + dt_bias).astype(f32))
    A = -jnp.exp(a_log.astype(f32))
    y = ssd_chunked(xs * dt[..., None], dt * A, Bm, Cm)
    y = y + xs * d_skip.astype(f32)[:, None]
    y = y.reshape(Bsz, T, SSD_WIDTH) * jax.nn.silu(z.astype(f32))
    y = y.reshape(Bsz, T, SSD_GROUPS, SSD_WIDTH // SSD_GROUPS)
    y = y * lax.rsqrt(jnp.mean(y * y, axis=-1, keepdims=True) + NORM_EPS)
    y = y.reshape(Bsz, T, SSD_WIDTH) * gnorm_w.astype(f32)
    return y.astype(h.dtype) @ w_out


def stick_breaking_mixer(h, p):
    w_in, q_norm, k_norm, w_out = p
    Bsz, T, _ = h.shape
    f32 = jnp.float32
    proj = h @ w_in
    W = SB_WIDTH
    q = rms_norm(proj[..., :W].reshape(Bsz, T, SB_HEADS, SB_HEAD), q_norm).astype(f32)
    k = rms_norm(proj[..., W:2 * W].reshape(Bsz, T, SB_HEADS, SB_HEAD), k_norm).astype(f32)
    v = proj[..., 2 * W:3 * W].reshape(Bsz, T, SB_HEADS, SB_HEAD).astype(f32)
    g = jax.nn.silu(proj[..., 3 * W:])
    outs = []
    for blk in range(T // SB_BLOCK):
        start, end = blk * SB_BLOCK, (blk + 1) * SB_BLOCK
        qb, kb, vb = q[:, start:end], k[:, :end], v[:, :end]
        z = jnp.einsum('bthd,bshd->bhts', qb, kb) * SB_SCALE
        t_idx = start + jnp.arange(SB_BLOCK)
        s_idx = jnp.arange(end)
        strict = s_idx[None, :] < t_idx[:, None]
        log_beta = jax.nn.log_sigmoid(z)
        log_1m_beta = jnp.where(strict, log_beta - z, 0.0)
        tail = lax.cumsum(log_1m_beta, axis=3, reverse=True) - log_1m_beta
        att = jnp.where(strict, jnp.exp(log_beta + tail), 0.0)
        outs.append(jnp.einsum('bhts,bshd->bthd', att, vb))
    o = jnp.concatenate(outs, axis=1).reshape(Bsz, T, W).astype(h.dtype)
    return (o * g) @ w_out


def _normal(key, shape, scale):
    return scale * jax.random.normal(key, shape, jnp.float32)


def _rwkv_params(key, prefix, value_residual):
    ks = jax.random.split(key, 16)
    W = RW_WIDTH
    n_cols = 4 * W + RW_DECAY_RANK + RW_AAA_RANK + (RW_VRES_RANK if value_residual else 0)
    p = {
        prefix + "norm": 1.0 + _normal(ks[0], (D_MODEL,), 0.05),
        prefix + "mu": jax.random.uniform(ks[1], (6, D_MODEL), jnp.float32),
        prefix + "w_in": _normal(ks[2], (D_MODEL, n_cols), D_MODEL ** -0.5),
        prefix + "w_up": _normal(ks[3], (RW_DECAY_RANK, W), 0.5 * RW_DECAY_RANK ** -0.5),
        prefix + "w0": jax.random.uniform(ks[4], (W,), jnp.float32, -6.0, 1.0),
        prefix + "a_up": _normal(ks[5], (RW_AAA_RANK, W), 0.5 * RW_AAA_RANK ** -0.5),
        prefix + "a0": _normal(ks[6], (W,), 0.5),
        prefix + "k_k": 0.85 + _normal(ks[7], (W,), 0.05),
        prefix + "k_a": 1.0 + _normal(ks[8], (W,), 0.05),
        prefix + "r_k": _normal(ks[9], (RW_HEADS, RW_HEAD), 0.1),
        prefix + "gn_w": 1.0 + _normal(ks[10], (W,), 0.05),
        prefix + "gn_b": _normal(ks[11], (W,), 0.01),
        prefix + "w_out": _normal(ks[12], (W, D_MODEL), W ** -0.5),
    }
    if value_residual:
        p[prefix + "v_up"] = _normal(ks[13], (RW_VRES_RANK, W), 0.5 * RW_VRES_RANK ** -0.5)
        p[prefix + "v0"] = _normal(ks[14], (W,), 0.5)
    return p


def _mamba2_params(key, prefix):
    ks = jax.random.split(key, 10)
    n_cols = SSD_WIDTH + SSD_CONV_DIM + SSD_HEADS
    dt = jnp.exp(jax.random.uniform(ks[4], (SSD_HEADS,), jnp.float32, math.log(1e-3), math.log(1e-1)))
    return {
        prefix + "norm": 1.0 + _normal(ks[0], (D_MODEL,), 0.05),
        prefix + "w_in": _normal(ks[1], (D_MODEL, n_cols), D_MODEL ** -0.5),
        prefix + "conv_w": _normal(ks[2], (SSD_CONV, SSD_CONV_DIM), SSD_CONV ** -0.5),
        prefix + "conv_b": _normal(ks[3], (SSD_CONV_DIM,), 0.01),
        prefix + "dt_bias": dt + jnp.log(-jnp.expm1(-dt)),
        prefix + "a_log": jnp.log(jax.random.uniform(ks[5], (SSD_HEADS,), jnp.float32, 1.0, 16.0)),
        prefix + "d_skip": 1.0 + _normal(ks[6], (SSD_HEADS,), 0.05),
        prefix + "gnorm_w": 1.0 + _normal(ks[7], (SSD_WIDTH,), 0.05),
        prefix + "w_out": _normal(ks[8], (SSD_WIDTH, D_MODEL), SSD_WIDTH ** -0.5),
    }


def _stickbreak_params(key, prefix):
    ks = jax.random.split(key, 5)
    return {
        prefix + "norm": 1.0 + _normal(ks[0], (D_MODEL,), 0.05),
        prefix + "w_in": _normal(ks[1], (D_MODEL, 4 * SB_WIDTH), D_MODEL ** -0.5),
        prefix + "q_norm": 1.0 + _normal(ks[2], (SB_HEAD,), 0.05),
        prefix + "k_norm": 1.0 + _normal(ks[3], (SB_HEAD,), 0.05),
        prefix + "w_out": _normal(ks[4], (SB_WIDTH, D_MODEL), SB_WIDTH ** -0.5),
    }


def setup_inputs(seed: int = 0) -> dict:
    key = jax.random.key(seed)
    kx, k0, k1, k2, k3 = jax.random.split(key, 5)
    inputs = {"x": jax.random.normal(kx, (BATCH, SEQ, D_MODEL), jnp.float32)}
    inputs.update(_rwkv_params(k0, "l0_", value_residual=False))
    inputs.update(_mamba2_params(k1, "l1_"))
    inputs.update(_stickbreak_params(k2, "l2_"))
    inputs.update(_rwkv_params(k3, "l3_", value_residual=True))
    return inputs


def reference(x,
              l0_norm, l0_mu, l0_w_in, l0_w_up, l0_w0, l0_a_up, l0_a0, l0_k_k, l0_k_a, l0_r_k,
              l0_gn_w, l0_gn_b, l0_w_out,
              l1_norm, l1_w_in, l1_conv_w, l1_conv_b, l1_dt_bias, l1_a_log, l1_d_skip, l1_gnorm_w,
              l1_w_out,
              l2_norm, l2_w_in, l2_q_norm, l2_k_norm, l2_w_out,
              l3_norm, l3_mu, l3_w_in, l3_w_up, l3_w0, l3_a_up, l3_a0, l3_k_k, l3_k_a, l3_r_k,
              l3_gn_w, l3_gn_b, l3_w_out, l3_v_up, l3_v0):
    layer_params = [
        (l0_norm, (l0_mu, l0_w_in, l0_w_up, l0_w0, l0_a_up, l0_a0, l0_k_k, l0_k_a, l0_r_k,
                   l0_gn_w, l0_gn_b, l0_w_out)),
        (l1_norm, (l1_w_in, l1_conv_w, l1_conv_b, l1_dt_bias, l1_a_log, l1_d_skip, l1_gnorm_w,
                   l1_w_out)),
        (l2_norm, (l2_w_in, l2_q_norm, l2_k_norm, l2_w_out)),
        (l3_norm, (l3_mu, l3_w_in, l3_w_up, l3_w0, l3_a_up, l3_a0, l3_k_k, l3_k_a, l3_r_k,
                   l3_gn_w, l3_gn_b, l3_w_out, l3_v_up, l3_v0)),
    ]
    v_first = None
    for i in range(DEPTH):
        norm_g, p = layer_params[i]
        h = rms_norm(x, norm_g)
        kind = i % N_MIXERS
        if kind == 0:
            y, v_first = rwkv7_mixer(h, p, v_first)
        elif kind == 1:
            y = mamba2_mixer(h, p)
        else:
            y = stick_breaking_mixer(h, p)
        x = x + y.astype(x.dtype)
    return x
```

```python
import functools
import math

import jax
import jax.numpy as jnp
from jax import lax
from jax.experimental import pallas as pl
from jax.experimental.pallas import tpu as pltpu

F32 = jnp.float32
BF16 = jnp.bfloat16

LANES = 128
HEAD = 64
NORM_EPS = 1e-6
RW_GN_EPS = 64e-5
RW_CHUNK = 64
SSD_CHUNK = 128
SSD_STATE = 128
SSD_GROUPS = 8
SSD_CONV = 4
SB_BLOCK = 128
VMEM_LIMIT = 56 * 1024 * 1024


def _cparams(*sem):
    return pltpu.CompilerParams(dimension_semantics=sem, vmem_limit_bytes=VMEM_LIMIT)


def _dot(a, b):
    return jnp.dot(a.astype(BF16), b.astype(BF16), preferred_element_type=F32)


def _dot_nt(a, b):
    return lax.dot_general(a.astype(BF16), b.astype(BF16), (((1,), (1,)), ((), ())),
                           preferred_element_type=F32)


def _dot_tn(a, b):
    return lax.dot_general(a.astype(BF16), b.astype(BF16), (((0,), (0,)), ((), ())),
                           preferred_element_type=F32)


def _split(x, terms):
    out = []
    rem = x
    for i in range(terms):
        p = rem.astype(BF16)
        out.append(p)
        if i + 1 < terms:
            rem = rem - p.astype(F32)
    return out


def _dot_sel_l(sel_bf16, x, terms=3):
    acc = None
    for p in _split(x, terms):
        d = jnp.dot(sel_bf16, p, preferred_element_type=F32)
        acc = d if acc is None else acc + d
    return acc


def _dot_sel_r(x, sel_bf16, terms=3):
    acc = None
    for p in _split(x, terms):
        d = jnp.dot(p, sel_bf16, preferred_element_type=F32)
        acc = d if acc is None else acc + d
    return acc


def _dot3(a, b):
    a_hi, a_lo = _split(a, 2)
    b_hi, b_lo = _split(b, 2)
    return (jnp.dot(a_hi, b_hi, preferred_element_type=F32)
            + jnp.dot(a_hi, b_lo, preferred_element_type=F32)
            + jnp.dot(a_lo, b_hi, preferred_element_type=F32))


def _rms(x, g):
    return x * lax.rsqrt(jnp.mean(x * x, axis=-1, keepdims=True) + NORM_EPS) * g


def _sigmoid(x):
    return 1.0 / (1.0 + jnp.exp(-x))


def _silu(x):
    return x * _sigmoid(x)


def _row_tile(t, target):
    tm = min(t, target)
    while t % tm:
        tm //= 2
    return tm


def _const_spec(shape):
    nd = len(shape)
    return pl.BlockSpec(shape, lambda *_: (0,) * nd)


def _rwkv_in_kernel(tiles_per_seq, has_vres, x_ref, xp_ref, g_ref, mu_ref, w4_ref, ws_ref,
                    wup_ref, w0_ref, aup_ref, a0_ref, *rest):
    if has_vres:
        wv_ref, vup_ref, v0_ref, vf_ref, r_o, w_o, k_o, v_o, a_o, g_o = rest
    else:
        r_o, w_o, k_o, v_o, a_o, g_o = rest
    width = r_o.shape[-1]
    g = g_ref[...]
    h = _rms(x_ref[...], g)
    hp = _rms(xp_ref[...], g)[7:8, :]
    first = (pl.program_id(0) % tiles_per_seq) == 0
    hp = jnp.where(first, 0.0, hp)
    rows = lax.broadcasted_iota(jnp.int32, h.shape, 0)
    h_prev = jnp.where(rows == 0, hp, pltpu.roll(h, 1, 0))
    dx = h_prev - h
    mu = mu_ref[...]

    def mix(i):
        return (h + dx * mu[i:i + 1, :]).astype(BF16)

    x_r, x_w, x_k, x_v, x_a, x_g = (mix(i) for i in range(6))
    r_o[...] = jnp.dot(x_r, w4_ref[:, 0:width], preferred_element_type=F32)
    k_o[...] = jnp.dot(x_k, w4_ref[:, width:2 * width], preferred_element_type=F32)
    v = jnp.dot(x_v, w4_ref[:, 2 * width:3 * width], preferred_element_type=F32)
    g_o[...] = _silu(jnp.dot(x_g, w4_ref[:, 3 * width:4 * width], preferred_element_type=F32))
    w_lo = jnp.tanh(jnp.dot(x_w, ws_ref[...], preferred_element_type=F32))
    a_lo = jnp.dot(x_a, ws_ref[...], preferred_element_type=F32)
    w_raw = w0_ref[...] + _dot(w_lo, wup_ref[...])
    w_o[...] = -_sigmoid(w_raw) * math.exp(-0.5)
    a_o[...] = _sigmoid(a0_ref[...] + _dot(a_lo, aup_ref[...]))
    if has_vres:
        v_lo = jnp.dot(x_v, wv_ref[...], preferred_element_type=F32)
        gate = _sigmoid(v0_ref[...] + _dot(v_lo, vup_ref[...]))
        v = v + (vf_ref[...] - v) * gate
    v_o[...] = v


def _rwkv_in(x2d, seq, norm_g, mu, w_in, w_up, w0, a_up, a0, v_up=None, v0=None, v_first=None):
    n, d = x2d.shape
    width = w_up.shape[1]
    rank_w, rank_a = w_up.shape[0], a_up.shape[0]
    has_vres = v_up is not None
    tm = _row_tile(seq, 256)
    tiles_per_seq = seq // tm
    w4 = w_in[:, :4 * width].astype(BF16)
    ws = w_in[:, 4 * width:4 * width + rank_w + rank_a].astype(BF16)
    wup = jnp.concatenate([w_up, jnp.zeros((rank_a, width), F32)], 0).astype(BF16)
    aup = jnp.concatenate([jnp.zeros((rank_w, width), F32), a_up], 0).astype(BF16)
    mu8 = jnp.concatenate([mu, jnp.zeros((2, d), F32)], 0)
    row = lambda p: p.reshape(1, -1)
    tile = pl.BlockSpec((tm, d), lambda i: (i, 0))
    wtile = pl.BlockSpec((tm, width), lambda i: (i, 0))
    prev = pl.BlockSpec((8, d), lambda i: (jnp.maximum(i * (tm // 8) - 1, 0), 0))
    args = [x2d, x2d, row(norm_g), mu8, w4, ws, wup, row(w0), aup, row(a0)]
    specs = [tile, prev, _const_spec((1, d)), _const_spec((8, d)), _const_spec(w4.shape),
             _const_spec(ws.shape), _const_spec(wup.shape), _const_spec((1, width)),
             _const_spec(aup.shape), _const_spec((1, width))]
    if has_vres:
        rank_v = v_up.shape[0]
        wv = jnp.pad(w_in[:, 4 * width + rank_w + rank_a:], ((0, 0), (0, LANES - rank_v))).astype(BF16)
        vup = jnp.pad(v_up, ((0, LANES - rank_v), (0, 0))).astype(BF16)
        args += [wv, vup, row(v0), v_first]
        specs += [_const_spec(wv.shape), _const_spec(vup.shape), _const_spec((1, width)), wtile]
    out_shape = [jax.ShapeDtypeStruct((n, width), F32)] * 6
    return pl.pallas_call(
        functools.partial(_rwkv_in_kernel, tiles_per_seq, has_vres),
        grid=(n // tm,),
        in_specs=specs,
        out_specs=[wtile] * 6,
        out_shape=out_shape,
        compiler_params=_cparams("parallel"),
        name="rwkv_in",
    )(*args)


def _rwkv_scan_kernel(r_ref, w_ref, k_ref, v_ref, a_ref, kk_p, ka_p, rk_p, gw_p, gb_p,
                      y_ref, st_ref):
    L = RW_CHUNK
    seq = r_ref.shape[1]
    n2 = 2 * L
    rows = lax.broadcasted_iota(jnp.int32, (n2, LANES), 0)
    cols = lax.broadcasted_iota(jnp.int32, (n2, LANES), 1)
    own = (rows < L) == (cols < HEAD)
    lower = rows > cols
    lower_eq = rows >= cols
    eye = (rows == cols).astype(F32)
    ri = lax.broadcasted_iota(jnp.int32, (L, L), 0)
    ci = lax.broadcasted_iota(jnp.int32, (L, L), 1)
    tril_incl = (ri >= ci).astype(BF16)
    k_k, k_a, r_k, gn_w, gn_b = kk_p[...], ka_p[...], rk_p[...], gw_p[...], gb_p[...]

    def dup(x):
        return jnp.concatenate([x, x], axis=0)

    def stack(x):
        return jnp.where(own, dup(x), 0.0)

    gw_bd = stack(jnp.broadcast_to(gn_w, (L, LANES)))
    gb_bd = stack(jnp.broadcast_to(gn_b, (L, LANES)))
    st_ref[...] = jnp.zeros_like(st_ref)

    def chunk(c, carry):
        sl = pl.ds(pl.multiple_of(c * L, L), L)
        r = r_ref[0, sl, :]
        lw = w_ref[0, sl, :]
        k = k_ref[0, sl, :]
        v = v_ref[0, sl, :]
        a = a_ref[0, sl, :]
        cw = _dot_sel_l(tril_incl, lw)
        cw_last = cw[L - 1:L, :]
        e_in = jnp.exp(cw)
        e_ex = jnp.exp(cw - lw)
        e_inv = jnp.exp(-cw)
        e_end = jnp.exp(cw_last - cw)
        kk_bd = stack(k * k_k)
        nrm = jnp.sqrt(jnp.sum(kk_bd * kk_bd, axis=-1, keepdims=True))
        kk_bd = kk_bd / jnp.maximum(nrm, 1e-12)
        k_mod = k * (1.0 + (a - 1.0) * k_a)
        at_bd = -kk_bd * dup(e_ex)
        bt_bd = kk_bd * dup(a * e_inv)
        kt_bd = stack(k_mod * e_inv)
        rt_bd = stack(r * e_in)
        v_bd = stack(v)
        bw_bd = kk_bd * dup(a * e_end)
        kw_bd = stack(k_mod * e_end)

        a_ab = jnp.where(lower, _dot_nt(at_bd, bt_bd), 0.0)
        a_ak = jnp.where(lower, _dot_nt(at_bd, kt_bd), 0.0)
        a_rb = jnp.where(lower_eq, _dot_nt(rt_bd, bt_bd), 0.0)
        a_rk = jnp.where(lower_eq, _dot_nt(rt_bd, kt_bd), 0.0)

        t_inv = eye + a_ab
        p = a_ab
        for _ in range(int(math.log2(L)) - 1):
            p = _dot3(p, p)
            t_inv = t_inv + _dot3(p, t_inv)

        st = st_ref[...]
        u = _dot3(t_inv, _dot_nt(at_bd, st) + _dot(a_ak, v_bd))
        y = _dot_nt(rt_bd, st) + _dot(a_rb, u) + _dot(a_rk, v_bd)
        st_ref[...] = st * jnp.exp(cw_last) + _dot_tn(u, bw_bd) + _dot_tn(v_bd, kw_bd)

        mean = jnp.sum(y, axis=-1, keepdims=True) * (1.0 / HEAD)
        d = jnp.where(own, y - mean, 0.0)
        var = jnp.sum(d * d, axis=-1, keepdims=True) * (1.0 / HEAD)
        yn = d * lax.rsqrt(var + RW_GN_EPS) * gw_bd + gb_bd
        bonus = jnp.sum(stack(r * k_mod * r_k), axis=-1, keepdims=True)
        yo = yn + bonus * v_bd
        y_ref[0, sl, :] = yo[:L] + yo[L:]
        return carry

    lax.fori_loop(0, seq // L, chunk, 0)


def _rwkv_scan(r, w, k, v, a, k_k, k_a, r_k, gn_w, gn_b):
    b, seq, width = r.shape
    pairs = width // LANES
    blk = pl.BlockSpec((1, seq, LANES), lambda i, j: (i, 0, j))
    par = pl.BlockSpec((1, LANES), lambda i, j: (0, j))
    row = lambda p: p.reshape(1, -1)
    return pl.pallas_call(
        _rwkv_scan_kernel,
        grid=(b, pairs),
        in_specs=[blk] * 5 + [par] * 5,
        out_specs=blk,
        out_shape=jax.ShapeDtypeStruct((b, seq, width), F32),
        scratch_shapes=[pltpu.VMEM((LANES, LANES), F32)],
        compiler_params=_cparams("parallel", "parallel"),
        name="rwkv_scan",
    )(r, w, k, v, a, row(k_k), row(k_a), row(r_k), row(gn_w), row(gn_b))


def _gated_out_kernel(x_ref, y_ref, g_ref, w_ref, o_ref):
    yg = (y_ref[...] * g_ref[...]).astype(BF16)
    o_ref[...] = x_ref[...] + jnp.dot(yg, w_ref[...], preferred_element_type=F32)


def _gated_out(x2d, y2d, g2d, w_out, seq):
    n, d = x2d.shape
    width = y2d.shape[1]
    tm = _row_tile(seq, 512)
    return pl.pallas_call(
        _gated_out_kernel,
        grid=(n // tm,),
        in_specs=[pl.BlockSpec((tm, d), lambda i: (i, 0)),
                  pl.BlockSpec((tm, width), lambda i: (i, 0)),
                  pl.BlockSpec((tm, width), lambda i: (i, 0)),
                  _const_spec((width, d))],
        out_specs=pl.BlockSpec((tm, d), lambda i: (i, 0)),
        out_shape=jax.ShapeDtypeStruct((n, d), F32),
        compiler_params=_cparams("parallel"),
        name="gated_out",
    )(x2d, y2d, g2d, w_out.astype(BF16))


def _rwkv_layer(x, norm_g, p, v_first):
    mu, w_in, w_up, w0, a_up, a0, k_k, k_a, r_k, gn_w, gn_b, w_out = p[:12]
    b, seq, d = x.shape
    x2d = x.reshape(b * seq, d)
    if v_first is None:
        r, w, k, v, a, g = _rwkv_in(x2d, seq, norm_g, mu, w_in, w_up, w0, a_up, a0)
        v_first = v
    else:
        v_up, v0 = p[12:]
        r, w, k, v, a, g = _rwkv_in(x2d, seq, norm_g, mu, w_in, w_up, w0, a_up, a0,
                                    v_up, v0, v_first)
    width = r.shape[1]
    sh = lambda t: t.reshape(b, seq, width)
    y = _rwkv_scan(sh(r), sh(w), sh(k), sh(v), sh(a), k_k, k_a, r_k.reshape(-1), gn_w, gn_b)
    out = _gated_out(x2d, y.reshape(b * seq, width), g, w_out, seq)
    return out.reshape(b, seq, d), v_first


def _norm_proj_kernel(acts, x_ref, g_ref, *refs):
    n = len(acts)
    w_refs, o_refs = refs[:n], refs[n:]
    h = _rms(x_ref[...], g_ref[...]).astype(BF16)
    for act, w_ref, o_ref in zip(acts, w_refs, o_refs):
        y = jnp.dot(h, w_ref[...], preferred_element_type=F32)
        o_ref[...] = _silu(y) if act else y


def _norm_proj(x2d, seq, norm_g, weights, acts):
    n, d = x2d.shape
    tm = _row_tile(seq, 256)
    weights = [w.astype(BF16) for w in weights]
    return pl.pallas_call(
        functools.partial(_norm_proj_kernel, tuple(acts)),
        grid=(n // tm,),
        in_specs=[pl.BlockSpec((tm, d), lambda i: (i, 0)), _const_spec((1, d))]
                 + [_const_spec(w.shape) for w in weights],
        out_specs=[pl.BlockSpec((tm, w.shape[1]), lambda i: (i, 0)) for w in weights],
        out_shape=[jax.ShapeDtypeStruct((n, w.shape[1]), F32) for w in weights],
        compiler_params=_cparams("parallel"),
        name="norm_proj",
    )(x2d, norm_g.reshape(1, d), *weights)


def _ssd_kernel(x_ref, z_ref, u_ref, dt_ref, cw_ref, cb_ref, dtb_ref, alog_ref, dsk_ref,
                gnw_ref, exp_ref, wout_ref, o_ref, ext_ref, h_ref, y_ref):
    L = SSD_CHUNK
    width = z_ref.shape[-1]
    heads_per_group = width // HEAD // SSD_GROUPS
    gw = heads_per_group * HEAD
    c = pl.program_id(1)

    @pl.when(c == 0)
    def _():
        ext_ref[0:8, :] = jnp.zeros((8, ext_ref.shape[1]), F32)
        h_ref[...] = jnp.zeros_like(h_ref)

    u = u_ref[0]
    ext_ref[8:8 + L, :] = u
    conv = cb_ref[...] + cw_ref[SSD_CONV - 1:SSD_CONV, :] * u
    for j in range(SSD_CONV - 1):
        conv = conv + cw_ref[j:j + 1, :] * ext_ref[pl.ds(8 - (SSD_CONV - 1) + j, L), :]
    ext_ref[0:8, :] = u[L - 8:, :]
    xbc = _silu(conv)
    xs = xbc[:, :width]
    n_bc = SSD_GROUPS * SSD_STATE
    bm = xbc[:, width:width + n_bc]
    cm = xbc[:, width + n_bc:]

    x_in = dt_ref[0] + dtb_ref[...]
    dt = jnp.maximum(x_in, 0.0) + jnp.log1p(jnp.exp(-jnp.abs(x_in)))
    a = dt * (-jnp.exp(alog_ref[...]))
    ri = lax.broadcasted_iota(jnp.int32, (L, L), 0)
    ci = lax.broadcasted_iota(jnp.int32, (L, L), 1)
    causal = ri >= ci
    ac = _dot_sel_l(causal.astype(BF16), a)
    ac_t = ac.T
    expand = exp_ref[...]
    dt_w = _dot_sel_r(dt, expand)
    ac_w = _dot_sel_r(ac, expand)
    ac_last_w = ac_w[L - 1:L, :]
    x_dt = xs * dt_w
    e_out = jnp.exp(ac_w)
    xd = x_dt * jnp.exp(ac_last_w - ac_w)
    e_last = jnp.exp(ac_last_w)
    lane_g = lax.broadcasted_iota(jnp.int32, (L, gw), 1) // HEAD

    for g in range(SSD_GROUPS):
        bg = bm[:, g * SSD_STATE:(g + 1) * SSD_STATE]
        cg = cm[:, g * SSD_STATE:(g + 1) * SSD_STATE]
        gs = slice(g * gw, (g + 1) * gw)
        cb = _dot_nt(cg, bg)
        h_prev = h_ref[g]
        y_g = _dot(cg, h_prev) * e_out[:, gs]
        x_g = x_dt[:, gs]
        for r in range(heads_per_group):
            hd = g * heads_per_group + r
            seg = ac[:, hd:hd + 1] - ac_t[hd:hd + 1, :]
            lm = cb * jnp.exp(jnp.where(causal, seg, -1e30))
            y_g = y_g + _dot(lm, jnp.where(lane_g == r, x_g, 0.0))
        h_ref[g] = h_prev * e_last[:, gs] + _dot_tn(bg, xd[:, gs])
        y_ref[:, gs] = y_g

    y = (y_ref[...] + xs * dsk_ref[...]) * _silu(z_ref[0])
    for g in range(SSD_GROUPS):
        gs = slice(g * gw, (g + 1) * gw)
        yg = y[:, gs]
        ms = jnp.mean(yg * yg, axis=-1, keepdims=True)
        y_ref[:, gs] = yg * lax.rsqrt(ms + NORM_EPS)
    yn = (y_ref[...] * gnw_ref[...]).astype(BF16)
    o_ref[0] = x_ref[0] + jnp.dot(yn, wout_ref[...], preferred_element_type=F32)


def _mamba_layer(x, norm_g, p):
    w_in, conv_w, conv_b, dt_bias, a_log, d_skip, gnorm_w, w_out = p
    b, seq, d = x.shape
    width = w_out.shape[0]
    heads = d_skip.shape[0]
    conv_dim = conv_w.shape[1]
    L = SSD_CHUNK
    x2d = x.reshape(b * seq, d)
    w_dt = jnp.pad(w_in[:, width + conv_dim:], ((0, 0), (0, LANES - heads)))
    z, u, dt = _norm_proj(x2d, seq, norm_g,
                          [w_in[:, :width], w_in[:, width:width + conv_dim], w_dt],
                          [False, False, False])
    pad_h = lambda v: jnp.pad(v, (0, LANES - heads)).reshape(1, LANES)
    expand = (jnp.arange(LANES)[:, None] == (jnp.arange(width) // HEAD)[None, :]).astype(BF16)
    cw8 = jnp.pad(conv_w, ((0, 8 - conv_w.shape[0]), (0, 0)))
    gw = width // SSD_GROUPS
    blk = lambda n: pl.BlockSpec((1, L, n), lambda i, j: (i, j, 0))
    out = pl.pallas_call(
        _ssd_kernel,
        grid=(b, seq // L),
        in_specs=[blk(d), blk(width), blk(conv_dim), blk(LANES),
                  _const_spec((8, conv_dim)), _const_spec((1, conv_dim)),
                  _const_spec((1, LANES)), _const_spec((1, LANES)),
                  _const_spec((1, width)), _const_spec((1, width)),
                  _const_spec((LANES, width)), _const_spec((width, d))],
        out_specs=blk(d),
        out_shape=jax.ShapeDtypeStruct((b, seq, d), F32),
        scratch_shapes=[pltpu.VMEM((8 + L, conv_dim), F32),
                        pltpu.VMEM((SSD_GROUPS, SSD_STATE, gw), F32),
                        pltpu.VMEM((L, width), F32)],
        compiler_params=_cparams("parallel", "arbitrary"),
        name="ssd",
    )(x, z.reshape(b, seq, width), u.reshape(b, seq, conv_dim), dt.reshape(b, seq, LANES),
      cw8, conv_b.reshape(1, -1), pad_h(dt_bias), pad_h(a_log),
      jnp.repeat(d_skip, HEAD).reshape(1, -1), gnorm_w.reshape(1, -1), expand,
      w_out.astype(BF16))
    return out


def _sb_kernel(q_ref, k_ref, v_ref, qn_ref, kn_ref, o_ref, ks_ref):
    blk = SB_BLOCK
    seq = k_ref.shape[1]
    i = pl.program_id(2)
    lane = lax.broadcasted_iota(jnp.int32, (blk, LANES), 1)
    head0 = lane < HEAD

    def head_rms(t, g):
        sq = t * t
        s0 = jnp.sum(jnp.where(head0, sq, 0.0), axis=-1, keepdims=True)
        s1 = jnp.sum(jnp.where(head0, 0.0, sq), axis=-1, keepdims=True)
        inv = jnp.where(head0, lax.rsqrt(s0 * (1.0 / HEAD) + NORM_EPS),
                        lax.rsqrt(s1 * (1.0 / HEAD) + NORM_EPS))
        return t * inv * g

    @pl.when(i == 0)
    def _():
        def norm_block(j, carry):
            sl = pl.ds(pl.multiple_of(j * blk, blk), blk)
            ks_ref[sl, :] = head_rms(k_ref[0, sl, :], kn_ref[...])
            return carry
        lax.fori_loop(0, seq // blk, norm_block, 0)

    q = head_rms(q_ref[0], qn_ref[...]) * (1.0 / math.sqrt(HEAD))
    q_bd = jnp.concatenate([jnp.where(head0, q, 0.0), jnp.where(head0, 0.0, q)], axis=0)
    rows = lax.broadcasted_iota(jnp.int32, (2 * blk, blk), 0) % blk
    cols = lax.broadcasted_iota(jnp.int32, (2 * blk, blk), 1)
    strict = rows > cols
    ji = lax.broadcasted_iota(jnp.int32, (blk, 2 * blk), 0)
    si = lax.broadcasted_iota(jnp.int32, (blk, 2 * blk), 1)
    suffix = jnp.logical_or(ji > si, si >= blk).astype(BF16)

    def body(jj, carry):
        run, acc = carry
        sl = pl.ds(pl.multiple_of((i - jj) * blk, blk), blk)
        z = _dot_nt(q_bd, ks_ref[sl, :])
        valid = jnp.logical_or(jj > 0, strict)
        l1p = jnp.log1p(jnp.exp(-jnp.abs(z)))
        log_beta = jnp.minimum(z, 0.0) - l1p
        log_1m = jnp.where(valid, jnp.minimum(-z, 0.0) - l1p, 0.0)
        sums = _dot_sel_r(log_1m, suffix)
        att = jnp.where(valid, jnp.exp(log_beta + sums[:, :blk] + run), 0.0)
        acc = acc + _dot(att, v_ref[0, sl, :])
        return run + sums[:, blk:], acc

    zeros = jnp.zeros((2 * blk, blk), F32)
    _, acc = lax.fori_loop(0, i + 1, body, (zeros, zeros))
    o_ref[0] = jnp.where(head0, acc[:blk], acc[blk:])


def _sb_layer(x, norm_g, p):
    w_in, q_norm, k_norm, w_out = p
    b, seq, d = x.shape
    width = w_out.shape[0]
    x2d = x.reshape(b * seq, d)
    q, k, v, g = _norm_proj(x2d, seq, norm_g, [w_in[:, i * width:(i + 1) * width] for i in range(4)],
                            [False, False, False, True])
    sh = lambda t: t.reshape(b, seq, width)
    pair = lambda t: jnp.tile(t, LANES // HEAD).reshape(1, LANES)
    full = pl.BlockSpec((1, seq, LANES), lambda bi, pi, qi: (bi, 0, pi))
    qblk = pl.BlockSpec((1, SB_BLOCK, LANES), lambda bi, pi, qi: (bi, qi, pi))
    o = pl.pallas_call(
        _sb_kernel,
        grid=(b, width // LANES, seq // SB_BLOCK),
        in_specs=[qblk, full, full, _const_spec((1, LANES)), _const_spec((1, LANES))],
        out_specs=qblk,
        out_shape=jax.ShapeDtypeStruct((b, seq, width), F32),
        scratch_shapes=[pltpu.VMEM((seq, LANES), F32)],
        compiler_params=_cparams("parallel", "parallel", "arbitrary"),
        name="sb_attn",
    )(sh(q), sh(k), sh(v), pair(q_norm), pair(k_norm))
    out = _gated_out(x2d, o.reshape(b * seq, width), g, w_out, seq)
    return out.reshape(b, seq, d)


def kernel(x, l0_norm, l0_mu, l0_w_in, l0_w_up, l0_w0, l0_a_up, l0_a0, l0_k_k, l0_k_a, l0_r_k, l0_gn_w, l0_gn_b, l0_w_out, l1_norm, l1_w_in, l1_conv_w, l1_conv_b, l1_dt_bias, l1_a_log, l1_d_skip, l1_gnorm_w, l1_w_out, l2_norm, l2_w_in, l2_q_norm, l2_k_norm, l2_w_out, l3_norm, l3_mu, l3_w_in, l3_w_up, l3_w0, l3_a_up, l3_a0, l3_k_k, l3_k_a, l3_r_k, l3_gn_w, l3_gn_b, l3_w_out, l3_v_up, l3_v0):
    p0 = (l0_mu, l0_w_in, l0_w_up, l0_w0, l0_a_up, l0_a0, l0_k_k, l0_k_a, l0_r_k,
          l0_gn_w, l0_gn_b, l0_w_out)
    p3 = (l3_mu, l3_w_in, l3_w_up, l3_w0, l3_a_up, l3_a0, l3_k_k, l3_k_a, l3_r_k,
          l3_gn_w, l3_gn_b, l3_w_out, l3_v_up, l3_v0)
    p1 = (l1_w_in, l1_conv_w, l1_conv_b, l1_dt_bias, l1_a_log, l1_d_skip, l1_gnorm_w, l1_w_out)
    p2 = (l2_w_in, l2_q_norm, l2_k_norm, l2_w_out)
    x, v_first = _rwkv_layer(x, l0_norm, p0, None)
    x = _mamba_layer(x, l1_norm, p1)
    x = _sb_layer(x, l2_norm, p2)
    x, _ = _rwkv_layer(x, l3_norm, p3, v_first)
    return x
```

```python
import functools
import math

import jax
import jax.numpy as jnp
from jax import lax
from jax.experimental import pallas as pl
from jax.experimental.pallas import tpu as pltpu

F32 = jnp.float32
BF16 = jnp.bfloat16

LANES = 128
HEAD = 64
NORM_EPS = 1e-6
RW_GN_EPS = 64e-5
RW_CHUNK = 64
RW_TILES = 8
RW_TIME_BLOCK = 512
SSD_CHUNK = 128
SSD_STATE = 128
SSD_GROUPS = 8
SSD_CONV = 4
SB_BLOCK = 256
SB_DEAD = 104.0
VMEM_LIMIT = 56 * 1024 * 1024


def _cparams(*sem):
    return pltpu.CompilerParams(dimension_semantics=sem, vmem_limit_bytes=VMEM_LIMIT)


def _dot(a, b):
    return jnp.dot(a.astype(BF16), b.astype(BF16), preferred_element_type=F32)


def _dot_nt(a, b):
    return lax.dot_general(a.astype(BF16), b.astype(BF16), (((1,), (1,)), ((), ())),
                           preferred_element_type=F32)


def _dot_tn(a, b):
    return lax.dot_general(a.astype(BF16), b.astype(BF16), (((0,), (0,)), ((), ())),
                           preferred_element_type=F32)


def _split(x, terms):
    out = []
    rem = x
    for i in range(terms):
        p = rem.astype(BF16)
        out.append(p)
        if i + 1 < terms:
            rem = rem - p.astype(F32)
    return out


def _dot_sel_l(sel_bf16, x, terms=3):
    acc = None
    for p in _split(x, terms):
        d = jnp.dot(sel_bf16, p, preferred_element_type=F32)
        acc = d if acc is None else acc + d
    return acc


def _dot_sel_r(x, sel_bf16, terms=3):
    acc = None
    for p in _split(x, terms):
        d = jnp.dot(p, sel_bf16, preferred_element_type=F32)
        acc = d if acc is None else acc + d
    return acc


def _dot3(a, b):
    a_hi, a_lo = _split(a, 2)
    b_hi, b_lo = _split(b, 2)
    return (jnp.dot(a_hi, b_hi, preferred_element_type=F32)
            + jnp.dot(a_hi, b_lo, preferred_element_type=F32)
            + jnp.dot(a_lo, b_hi, preferred_element_type=F32))


def _rms(x, g):
    return x * lax.rsqrt(jnp.mean(x * x, axis=-1, keepdims=True) + NORM_EPS) * g


def _sigmoid(x):
    return 1.0 / (1.0 + jnp.exp(-x))


def _silu(x):
    return x * _sigmoid(x)


def _row_tile(t, target):
    tm = min(t, target)
    while t % tm:
        tm //= 2
    return tm


def _const_spec(shape):
    nd = len(shape)
    return pl.BlockSpec(shape, lambda *_: (0,) * nd)


def _rwkv_in_kernel(tiles_per_seq, has_vres, x_ref, xp_ref, g_ref, mu_ref, w4_ref, ws_ref,
                    wup_ref, w0_ref, aup_ref, a0_ref, *rest):
    if has_vres:
        wv_ref, vup_ref, v0_ref, vf_ref, r_o, w_o, k_o, v_o, a_o, g_o = rest
    else:
        r_o, w_o, k_o, v_o, a_o, g_o = rest
    width = r_o.shape[-1]
    g = g_ref[...]
    h = _rms(x_ref[...], g)
    hp = _rms(xp_ref[...], g)[7:8, :]
    first = (pl.program_id(0) % tiles_per_seq) == 0
    hp = jnp.where(first, 0.0, hp)
    rows = lax.broadcasted_iota(jnp.int32, h.shape, 0)
    h_prev = jnp.where(rows == 0, hp, pltpu.roll(h, 1, 0))
    dx = h_prev - h
    mu = mu_ref[...]

    def mix(i):
        return (h + dx * mu[i:i + 1, :]).astype(BF16)

    x_r, x_w, x_k, x_v, x_a, x_g = (mix(i) for i in range(6))
    r_o[...] = jnp.dot(x_r, w4_ref[:, 0:width], preferred_element_type=F32)
    k_o[...] = jnp.dot(x_k, w4_ref[:, width:2 * width], preferred_element_type=F32)
    v = jnp.dot(x_v, w4_ref[:, 2 * width:3 * width], preferred_element_type=F32)
    g_o[...] = _silu(jnp.dot(x_g, w4_ref[:, 3 * width:4 * width], preferred_element_type=F32))
    w_lo = jnp.tanh(jnp.dot(x_w, ws_ref[...], preferred_element_type=F32))
    a_lo = jnp.dot(x_a, ws_ref[...], preferred_element_type=F32)
    w_raw = w0_ref[...] + _dot(w_lo, wup_ref[...])
    w_o[...] = -_sigmoid(w_raw) * math.exp(-0.5)
    a_o[...] = _sigmoid(a0_ref[...] + _dot(a_lo, aup_ref[...]))
    if has_vres:
        v_lo = jnp.dot(x_v, wv_ref[...], preferred_element_type=F32)
        gate = _sigmoid(v0_ref[...] + _dot(v_lo, vup_ref[...]))
        v = v + (vf_ref[...] - v) * gate
    v_o[...] = v


def _rwkv_in(x2d, seq, norm_g, mu, w_in, w_up, w0, a_up, a0, v_up=None, v0=None, v_first=None):
    n, d = x2d.shape
    width = w_up.shape[1]
    rank_w, rank_a = w_up.shape[0], a_up.shape[0]
    has_vres = v_up is not None
    tm = _row_tile(seq, 256)
    tiles_per_seq = seq // tm
    w4 = w_in[:, :4 * width].astype(BF16)
    ws = w_in[:, 4 * width:4 * width + rank_w + rank_a].astype(BF16)
    wup = jnp.concatenate([w_up, jnp.zeros((rank_a, width), F32)], 0).astype(BF16)
    aup = jnp.concatenate([jnp.zeros((rank_w, width), F32), a_up], 0).astype(BF16)
    mu8 = jnp.concatenate([mu, jnp.zeros((2, d), F32)], 0)
    row = lambda p: p.reshape(1, -1)
    tile = pl.BlockSpec((tm, d), lambda i: (i, 0))
    wtile = pl.BlockSpec((tm, width), lambda i: (i, 0))
    prev = pl.BlockSpec((8, d), lambda i: (jnp.maximum(i * (tm // 8) - 1, 0), 0))
    args = [x2d, x2d, row(norm_g), mu8, w4, ws, wup, row(w0), aup, row(a0)]
    specs = [tile, prev, _const_spec((1, d)), _const_spec((8, d)), _const_spec(w4.shape),
             _const_spec(ws.shape), _const_spec(wup.shape), _const_spec((1, width)),
             _const_spec(aup.shape), _const_spec((1, width))]
    if has_vres:
        rank_v = v_up.shape[0]
        wv = jnp.pad(w_in[:, 4 * width + rank_w + rank_a:], ((0, 0), (0, LANES - rank_v))).astype(BF16)
        vup = jnp.pad(v_up, ((0, LANES - rank_v), (0, 0))).astype(BF16)
        args += [wv, vup, row(v0), v_first]
        specs += [_const_spec(wv.shape), _const_spec(vup.shape), _const_spec((1, width)), wtile]
    out_shape = [jax.ShapeDtypeStruct((n, width), F32)] * 6
    return pl.pallas_call(
        functools.partial(_rwkv_in_kernel, tiles_per_seq, has_vres),
        grid=(n // tm,),
        in_specs=specs,
        out_specs=[wtile] * 6,
        out_shape=out_shape,
        compiler_params=_cparams("parallel"),
        name="rwkv_in",
    )(*args)


def _rwkv_scan_kernel(r_ref, w_ref, k_ref, v_ref, a_ref, kk_p, ka_p, rk_p, gw_p, gb_p,
                      y_ref, st_ref):
    L = RW_CHUNK
    tb = r_ref.shape[1]
    tiles = r_ref.shape[2] // LANES
    n2 = 2 * L
    rows = lax.broadcasted_iota(jnp.int32, (n2, LANES), 0)
    cols = lax.broadcasted_iota(jnp.int32, (n2, LANES), 1)
    own = (rows < L) == (cols < HEAD)
    lower = rows > cols
    lower_eq = rows >= cols
    eye = (rows == cols).astype(F32)
    ri = lax.broadcasted_iota(jnp.int32, (L, L), 0)
    ci = lax.broadcasted_iota(jnp.int32, (L, L), 1)
    tril_incl = (ri >= ci).astype(BF16)

    def dup(x):
        return jnp.concatenate([x, x], axis=0)

    def stack(x):
        return jnp.where(own, dup(x), 0.0)

    @pl.when(pl.program_id(2) == 0)
    def _():
        st_ref[...] = jnp.zeros_like(st_ref)

    def one_tile(p, sl, cw, lw):
        ls = slice(p * LANES, (p + 1) * LANES)
        k_k, k_a, r_k = kk_p[:, ls], ka_p[:, ls], rk_p[:, ls]
        r = r_ref[0, sl, ls]
        k = k_ref[0, sl, ls]
        v = v_ref[0, sl, ls]
        a = a_ref[0, sl, ls]
        cw_last = cw[L - 1:L, :]
        e_in = jnp.exp(cw)
        e_ex = jnp.exp(cw - lw)
        e_inv = jnp.exp(-cw)
        e_end = jnp.exp(cw_last - cw)
        kk_bd = stack(k * k_k)
        nrm = jnp.sqrt(jnp.sum(kk_bd * kk_bd, axis=-1, keepdims=True))
        kk_bd = kk_bd * (1.0 / jnp.maximum(nrm, 1e-12))
        k_mod = k * (1.0 + (a - 1.0) * k_a)
        at_bd = (-kk_bd * dup(e_ex)).astype(BF16)
        bt_bd = (kk_bd * dup(a * e_inv)).astype(BF16)
        kt_bd = stack(k_mod * e_inv).astype(BF16)
        rt_bd = stack(r * e_in).astype(BF16)
        v_bd = stack(v)
        v_bf = v_bd.astype(BF16)
        bw_bd = (kk_bd * dup(a * e_end)).astype(BF16)
        kw_bd = stack(k_mod * e_end).astype(BF16)

        m = _dot_nt(jnp.concatenate([at_bd, rt_bd], axis=0),
                    jnp.concatenate([bt_bd, kt_bd], axis=0))
        yield
        a_ab = jnp.where(lower, m[:n2, :n2], 0.0)
        a_ak = jnp.where(lower, m[:n2, n2:], 0.0).astype(BF16)
        a_r = jnp.where(jnp.concatenate([lower_eq, lower_eq], axis=1), m[n2:, :], 0.0).astype(BF16)

        x = eye + a_ab
        pw = a_ab.astype(BF16)
        pw = _dot(pw, pw)
        yield
        pw = pw.astype(BF16)
        steps = int(math.log2(L)) - 1
        for s in range(steps):
            if s + 1 < steps:
                px = _dot(pw, jnp.concatenate([pw, x.astype(BF16)], axis=1))
                yield
                pw = px[:, :n2].astype(BF16)
                x = x + px[:, n2:]
            else:
                px = _dot(pw, x)
                yield
                x = x + px

        st = st_ref[p]
        rhs = _dot_nt(at_bd, st) + _dot(a_ak, v_bf)
        y0 = _dot_nt(rt_bd, st)
        yield
        u = _dot(x, rhs)
        yield
        uv = jnp.concatenate([u.astype(BF16), v_bf], axis=0)
        y = y0 + _dot(a_r, uv)
        st_ref[p] = st * jnp.exp(cw_last) + _dot_tn(uv, jnp.concatenate([bw_bd, kw_bd], axis=0))
        yield

        gw_bd = stack(jnp.broadcast_to(gw_p[:, ls], (L, LANES)))
        gb_bd = stack(jnp.broadcast_to(gb_p[:, ls], (L, LANES)))
        mean = jnp.sum(y, axis=-1, keepdims=True) * (1.0 / HEAD)
        d = jnp.where(own, y - mean, 0.0)
        var = jnp.sum(d * d, axis=-1, keepdims=True) * (1.0 / HEAD)
        yn = d * lax.rsqrt(var + RW_GN_EPS) * gw_bd + gb_bd
        bonus = jnp.sum(stack(r * k_mod * r_k), axis=-1, keepdims=True)
        yo = yn + bonus * v_bd
        y_ref[0, sl, ls] = yo[:L] + yo[L:]

    def chunk(c, carry):
        sl = pl.ds(pl.multiple_of(c * L, L), L)
        lw = w_ref[0, sl, :]
        cw = _dot_sel_l(tril_incl, lw)
        running = [one_tile(p, sl, cw[:, p * LANES:(p + 1) * LANES], lw[:, p * LANES:(p + 1) * LANES])
                   for p in range(tiles)]
        while running:
            running = [t for t in running if next(t, "done") != "done"]
        return carry

    lax.fori_loop(0, tb // L, chunk, 0)


def _rwkv_scan(r, w, k, v, a, k_k, k_a, r_k, gn_w, gn_b):
    b, seq, width = r.shape
    gl = RW_TILES * LANES
    tb = _row_tile(seq, RW_TIME_BLOCK)
    blk = pl.BlockSpec((1, tb, gl), lambda i, j, t: (i, t, j))
    par = pl.BlockSpec((1, gl), lambda i, j, t: (0, j))
    row = lambda p: p.reshape(1, -1)
    return pl.pallas_call(
        _rwkv_scan_kernel,
        grid=(b, width // gl, seq // tb),
        in_specs=[blk] * 5 + [par] * 5,
        out_specs=blk,
        out_shape=jax.ShapeDtypeStruct((b, seq, width), F32),
        scratch_shapes=[pltpu.VMEM((RW_TILES, LANES, LANES), F32)],
        compiler_params=_cparams("parallel", "parallel", "arbitrary"),
        name="rwkv_scan",
    )(r, w, k, v, a, row(k_k), row(k_a), row(r_k), row(gn_w), row(gn_b))


def _gated_out_kernel(x_ref, y_ref, g_ref, w_ref, o_ref):
    yg = (y_ref[...] * g_ref[...]).astype(BF16)
    o_ref[...] = x_ref[...] + jnp.dot(yg, w_ref[...], preferred_element_type=F32)


def _gated_out(x2d, y2d, g2d, w_out, seq):
    n, d = x2d.shape
    width = y2d.shape[1]
    tm = _row_tile(seq, 512)
    return pl.pallas_call(
        _gated_out_kernel,
        grid=(n // tm,),
        in_specs=[pl.BlockSpec((tm, d), lambda i: (i, 0)),
                  pl.BlockSpec((tm, width), lambda i: (i, 0)),
                  pl.BlockSpec((tm, width), lambda i: (i, 0)),
                  _const_spec((width, d))],
        out_specs=pl.BlockSpec((tm, d), lambda i: (i, 0)),
        out_shape=jax.ShapeDtypeStruct((n, d), F32),
        compiler_params=_cparams("parallel"),
        name="gated_out",
    )(x2d, y2d, g2d, w_out.astype(BF16))


def _rwkv_layer(x, norm_g, p, v_first):
    mu, w_in, w_up, w0, a_up, a0, k_k, k_a, r_k, gn_w, gn_b, w_out = p[:12]
    b, seq, d = x.shape
    x2d = x.reshape(b * seq, d)
    if v_first is None:
        r, w, k, v, a, g = _rwkv_in(x2d, seq, norm_g, mu, w_in, w_up, w0, a_up, a0)
        v_first = v
    else:
        v_up, v0 = p[12:]
        r, w, k, v, a, g = _rwkv_in(x2d, seq, norm_g, mu, w_in, w_up, w0, a_up, a0,
                                    v_up, v0, v_first)
    width = r.shape[1]
    sh = lambda t: t.reshape(b, seq, width)
    y = _rwkv_scan(sh(r), sh(w), sh(k), sh(v), sh(a), k_k, k_a, r_k.reshape(-1), gn_w, gn_b)
    out = _gated_out(x2d, y.reshape(b * seq, width), g, w_out, seq)
    return out.reshape(b, seq, d), v_first


def _norm_proj_kernel(acts, x_ref, g_ref, *refs):
    n = len(acts)
    w_refs, o_refs = refs[:n], refs[n:]
    h = _rms(x_ref[...], g_ref[...]).astype(BF16)
    for act, w_ref, o_ref in zip(acts, w_refs, o_refs):
        y = jnp.dot(h, w_ref[...], preferred_element_type=F32)
        o_ref[...] = _silu(y) if act else y


def _norm_proj(x2d, seq, norm_g, weights, acts):
    n, d = x2d.shape
    tm = _row_tile(seq, 256)
    weights = [w.astype(BF16) for w in weights]
    return pl.pallas_call(
        functools.partial(_norm_proj_kernel, tuple(acts)),
        grid=(n // tm,),
        in_specs=[pl.BlockSpec((tm, d), lambda i: (i, 0)), _const_spec((1, d))]
                 + [_const_spec(w.shape) for w in weights],
        out_specs=[pl.BlockSpec((tm, w.shape[1]), lambda i: (i, 0)) for w in weights],
        out_shape=[jax.ShapeDtypeStruct((n, w.shape[1]), F32) for w in weights],
        compiler_params=_cparams("parallel"),
        name="norm_proj",
    )(x2d, norm_g.reshape(1, d), *weights)


def _ssd_kernel(x_ref, z_ref, u_ref, dt_ref, cw_ref, cb_ref, dtb_ref, alog_ref, dsk_ref,
                gnw_ref, exp_ref, wout_ref, o_ref, ext_ref, h_ref, y_ref):
    L = SSD_CHUNK
    width = z_ref.shape[-1]
    heads_per_group = width // HEAD // SSD_GROUPS
    gw = heads_per_group * HEAD
    c = pl.program_id(1)

    @pl.when(c == 0)
    def _():
        ext_ref[0:8, :] = jnp.zeros((8, ext_ref.shape[1]), F32)
        h_ref[...] = jnp.zeros_like(h_ref)

    u = u_ref[0]
    ext_ref[8:8 + L, :] = u
    conv = cb_ref[...] + cw_ref[SSD_CONV - 1:SSD_CONV, :] * u
    for j in range(SSD_CONV - 1):
        conv = conv + cw_ref[j:j + 1, :] * ext_ref[pl.ds(8 - (SSD_CONV - 1) + j, L), :]
    ext_ref[0:8, :] = u[L - 8:, :]
    xbc = _silu(conv)
    xs = xbc[:, :width]
    n_bc = SSD_GROUPS * SSD_STATE
    bm = xbc[:, width:width + n_bc]
    cm = xbc[:, width + n_bc:]

    x_in = dt_ref[0] + dtb_ref[...]
    dt = jnp.maximum(x_in, 0.0) + jnp.log1p(jnp.exp(-jnp.abs(x_in)))
    a = dt * (-jnp.exp(alog_ref[...]))
    ri = lax.broadcasted_iota(jnp.int32, (L, L), 0)
    ci = lax.broadcasted_iota(jnp.int32, (L, L), 1)
    causal = ri >= ci
    ac = _dot_sel_l(causal.astype(BF16), a)
    ac_t = ac.T
    expand = exp_ref[...]
    dt_w = _dot_sel_r(dt, expand)
    ac_w = _dot_sel_r(ac, expand)
    ac_last_w = ac_w[L - 1:L, :]
    x_dt = xs * dt_w
    e_out = jnp.exp(ac_w)
    xd = x_dt * jnp.exp(ac_last_w - ac_w)
    e_last = jnp.exp(ac_last_w)
    lane_g = lax.broadcasted_iota(jnp.int32, (L, gw), 1) // HEAD

    for g in range(SSD_GROUPS):
        bg = bm[:, g * SSD_STATE:(g + 1) * SSD_STATE]
        cg = cm[:, g * SSD_STATE:(g + 1) * SSD_STATE]
        gs = slice(g * gw, (g + 1) * gw)
        cb = _dot_nt(cg, bg)
        h_prev = h_ref[g]
        y_g = _dot(cg, h_prev) * e_out[:, gs]
        x_g = x_dt[:, gs]
        for r in range(heads_per_group):
            hd = g * heads_per_group + r
            seg = ac[:, hd:hd + 1] - ac_t[hd:hd + 1, :]
            lm = cb * jnp.exp(jnp.where(causal, seg, -1e30))
            y_g = y_g + _dot(lm, jnp.where(lane_g == r, x_g, 0.0))
        h_ref[g] = h_prev * e_last[:, gs] + _dot_tn(bg, xd[:, gs])
        y_ref[:, gs] = y_g

    y = (y_ref[...] + xs * dsk_ref[...]) * _silu(z_ref[0])
    for g in range(SSD_GROUPS):
        gs = slice(g * gw, (g + 1) * gw)
        yg = y[:, gs]
        ms = jnp.mean(yg * yg, axis=-1, keepdims=True)
        y_ref[:, gs] = yg * lax.rsqrt(ms + NORM_EPS)
    yn = (y_ref[...] * gnw_ref[...]).astype(BF16)
    o_ref[0] = x_ref[0] + jnp.dot(yn, wout_ref[...], preferred_element_type=F32)


def _mamba_layer(x, norm_g, p):
    w_in, conv_w, conv_b, dt_bias, a_log, d_skip, gnorm_w, w_out = p
    b, seq, d = x.shape
    width = w_out.shape[0]
    heads = d_skip.shape[0]
    conv_dim = conv_w.shape[1]
    L = SSD_CHUNK
    x2d = x.reshape(b * seq, d)
    w_dt = jnp.pad(w_in[:, width + conv_dim:], ((0, 0), (0, LANES - heads)))
    z, u, dt = _norm_proj(x2d, seq, norm_g,
                          [w_in[:, :width], w_in[:, width:width + conv_dim], w_dt],
                          [False, False, False])
    pad_h = lambda v: jnp.pad(v, (0, LANES - heads)).reshape(1, LANES)
    expand = (jnp.arange(LANES)[:, None] == (jnp.arange(width) // HEAD)[None, :]).astype(BF16)
    cw8 = jnp.pad(conv_w, ((0, 8 - conv_w.shape[0]), (0, 0)))
    gw = width // SSD_GROUPS
    blk = lambda n: pl.BlockSpec((1, L, n), lambda i, j: (i, j, 0))
    out = pl.pallas_call(
        _ssd_kernel,
        grid=(b, seq // L),
        in_specs=[blk(d), blk(width), blk(conv_dim), blk(LANES),
                  _const_spec((8, conv_dim)), _const_spec((1, conv_dim)),
                  _const_spec((1, LANES)), _const_spec((1, LANES)),
                  _const_spec((1, width)), _const_spec((1, width)),
                  _const_spec((LANES, width)), _const_spec((width, d))],
        out_specs=blk(d),
        out_shape=jax.ShapeDtypeStruct((b, seq, d), F32),
        scratch_shapes=[pltpu.VMEM((8 + L, conv_dim), F32),
                        pltpu.VMEM((SSD_GROUPS, SSD_STATE, gw), F32),
                        pltpu.VMEM((L, width), F32)],
        compiler_params=_cparams("parallel", "arbitrary"),
        name="ssd",
    )(x, z.reshape(b, seq, width), u.reshape(b, seq, conv_dim), dt.reshape(b, seq, LANES),
      cw8, conv_b.reshape(1, -1), pad_h(dt_bias), pad_h(a_log),
      jnp.repeat(d_skip, HEAD).reshape(1, -1), gnorm_w.reshape(1, -1), expand,
      w_out.astype(BF16))
    return out


def _sb_kernel(q_ref, k_ref, v_ref, qn_ref, kn_ref, o_ref, ks_ref, vs_ref):
    blk = q_ref.shape[1]
    seq = k_ref.shape[1]
    i = pl.program_id(2)
    lane = lax.broadcasted_iota(jnp.int32, (blk, LANES), 1)
    head0 = lane < HEAD

    def head_rms(t, g):
        sq = t * t
        s0 = jnp.sum(jnp.where(head0, sq, 0.0), axis=-1, keepdims=True)
        s1 = jnp.sum(jnp.where(head0, 0.0, sq), axis=-1, keepdims=True)
        inv = jnp.where(head0, lax.rsqrt(s0 * (1.0 / HEAD) + NORM_EPS),
                        lax.rsqrt(s1 * (1.0 / HEAD) + NORM_EPS))
        return t * inv * g

    @pl.when(i == 0)
    def _():
        def norm_block(j, carry):
            sl = pl.ds(pl.multiple_of(j * blk, blk), blk)
            ks_ref[sl, :] = head_rms(k_ref[0, sl, :], kn_ref[...]).astype(BF16)
            vs_ref[sl, :] = v_ref[0, sl, :].astype(BF16)
            return carry
        lax.fori_loop(0, seq // blk, norm_block, 0)

    q = head_rms(q_ref[0], qn_ref[...]) * (1.0 / math.sqrt(HEAD))
    q_bd = jnp.concatenate([jnp.where(head0, q, 0.0), jnp.where(head0, 0.0, q)],
                           axis=0).astype(BF16)
    rows = lax.broadcasted_iota(jnp.int32, (2 * blk, blk), 0) % blk
    cols = lax.broadcasted_iota(jnp.int32, (2 * blk, blk), 1)
    strict = rows > cols
    ji = lax.broadcasted_iota(jnp.int32, (blk, blk), 0)
    si = lax.broadcasted_iota(jnp.int32, (blk, blk), 1)
    suffix = (ji > si).astype(BF16)

    def tile(j, run, diagonal):
        sl = pl.ds(pl.multiple_of(j * blk, blk), blk)
        z = _dot_nt(q_bd, ks_ref[sl, :])
        l1p = jnp.log1p(jnp.exp(-jnp.abs(z)))
        log_beta = jnp.minimum(z, 0.0) - l1p
        log_1m = jnp.minimum(-z, 0.0) - l1p
        if diagonal:
            log_1m = jnp.where(strict, log_1m, 0.0)
        tail = _dot_sel_r(log_1m, suffix, terms=2)
        att = jnp.exp(log_beta + tail + run)
        if diagonal:
            att = jnp.where(strict, att, 0.0)
        pv = _dot(att, vs_ref[sl, :])
        return run + jnp.sum(log_1m, axis=-1, keepdims=True), pv

    def live(run):
        return (jnp.max(run) > -SB_DEAD).astype(jnp.int32)

    run, acc = tile(i, jnp.zeros((2 * blk, 1), F32), True)

    def cond(c):
        jj, _, _, go = c
        return jnp.logical_and(jj <= i, go > 0)

    def body(c):
        jj, run, acc, _ = c
        run, pv = tile(i - jj, run, False)
        return jj + 1, run, acc + pv, live(run)

    _, _, acc, _ = lax.while_loop(cond, body, (jnp.int32(1), run, acc, live(run)))
    o_ref[0] = jnp.where(head0, acc[:blk], acc[blk:])


def _sb_layer(x, norm_g, p):
    w_in, q_norm, k_norm, w_out = p
    b, seq, d = x.shape
    width = w_out.shape[0]
    x2d = x.reshape(b * seq, d)
    q, k, v, g = _norm_proj(x2d, seq, norm_g, [w_in[:, i * width:(i + 1) * width] for i in range(4)],
                            [False, False, False, True])
    sh = lambda t: t.reshape(b, seq, width)
    pair = lambda t: jnp.tile(t, LANES // HEAD).reshape(1, LANES)
    full = pl.BlockSpec((1, seq, LANES), lambda bi, pi, qi: (bi, 0, pi))
    blk = _row_tile(seq, SB_BLOCK)
    qblk = pl.BlockSpec((1, blk, LANES), lambda bi, pi, qi: (bi, qi, pi))
    o = pl.pallas_call(
        _sb_kernel,
        grid=(b, width // LANES, seq // blk),
        in_specs=[qblk, full, full, _const_spec((1, LANES)), _const_spec((1, LANES))],
        out_specs=qblk,
        out_shape=jax.ShapeDtypeStruct((b, seq, width), F32),
        scratch_shapes=[pltpu.VMEM((seq, LANES), BF16), pltpu.VMEM((seq, LANES), BF16)],
        compiler_params=_cparams("parallel", "parallel", "arbitrary"),
        name="sb_attn",
    )(sh(q), sh(k), sh(v), pair(q_norm), pair(k_norm))
    out = _gated_out(x2d, o.reshape(b * seq, width), g, w_out, seq)
    return out.reshape(b, seq, d)


def kernel(x, l0_norm, l0_mu, l0_w_in, l0_w_up, l0_w0, l0_a_up, l0_a0, l0_k_k, l0_k_a, l0_r_k, l0_gn_w, l0_gn_b, l0_w_out, l1_norm, l1_w_in, l1_conv_w, l1_conv_b, l1_dt_bias, l1_a_log, l1_d_skip, l1_gnorm_w, l1_w_out, l2_norm, l2_w_in, l2_q_norm, l2_k_norm, l2_w_out, l3_norm, l3_mu, l3_w_in, l3_w_up, l3_w0, l3_a_up, l3_a0, l3_k_k, l3_k_a, l3_r_k, l3_gn_w, l3_gn_b, l3_w_out, l3_v_up, l3_v0):
    p0 = (l0_mu, l0_w_in, l0_w_up, l0_w0, l0_a_up, l0_a0, l0_k_k, l0_k_a, l0_r_k,
          l0_gn_w, l0_gn_b, l0_w_out)
    p3 = (l3_mu, l3_w_in, l3_w_up, l3_w0, l3_a_up, l3_a0, l3_k_k, l3_k_a, l3_r_k,
          l3_gn_w, l3_gn_b, l3_w_out, l3_v_up, l3_v0)
    p1 = (l1_w_in, l1_conv_w, l1_conv_b, l1_dt_bias, l1_a_log, l1_d_skip, l1_gnorm_w, l1_w_out)
    p2 = (l2_w_in, l2_q_norm, l2_k_norm, l2_w_out)
    x, v_first = _rwkv_layer(x, l0_norm, p0, None)
    x = _mamba_layer(x, l1_norm, p1)
    x = _sb_layer(x, l2_norm, p2)
    x, _ = _rwkv_layer(x, l3_norm, p3, v_first)
    return x
```

```python
import functools
import math

import jax
import jax.numpy as jnp
from jax import lax
from jax.experimental import pallas as pl
from jax.experimental.pallas import tpu as pltpu

F32 = jnp.float32
BF16 = jnp.bfloat16

LANES = 128
HEAD = 64
NORM_EPS = 1e-6
RW_GN_EPS = 64e-5
RW_CHUNK = 64
RW_TILES = 8
RW_BATCH = 2
RW_TIME_BLOCK = 256
SSD_CHUNK = 128
SSD_STATE = 128
SSD_GROUPS = 8
SSD_CONV = 4
SB_BLOCK = 256
SB_ROWS = 128
SB_DEAD = 104.0
VMEM_LIMIT = 56 * 1024 * 1024


def _cparams(*sem):
    return pltpu.CompilerParams(dimension_semantics=sem, vmem_limit_bytes=VMEM_LIMIT)


def _dot(a, b):
    return jnp.dot(a.astype(BF16), b.astype(BF16), preferred_element_type=F32)


def _dot_nt(a, b):
    return lax.dot_general(a.astype(BF16), b.astype(BF16), (((1,), (1,)), ((), ())),
                           preferred_element_type=F32)


def _dot_tn(a, b):
    return lax.dot_general(a.astype(BF16), b.astype(BF16), (((0,), (0,)), ((), ())),
                           preferred_element_type=F32)


def _split(x, terms):
    out = []
    rem = x
    for i in range(terms):
        p = rem.astype(BF16)
        out.append(p)
        if i + 1 < terms:
            rem = rem - p.astype(F32)
    return out


def _dot_sel_l(sel_bf16, x, terms=3):
    acc = None
    for p in _split(x, terms):
        d = jnp.dot(sel_bf16, p, preferred_element_type=F32)
        acc = d if acc is None else acc + d
    return acc


def _dot_sel_r(x, sel_bf16, terms=3):
    acc = None
    for p in _split(x, terms):
        d = jnp.dot(p, sel_bf16, preferred_element_type=F32)
        acc = d if acc is None else acc + d
    return acc


def _dot3(a, b):
    a_hi, a_lo = _split(a, 2)
    b_hi, b_lo = _split(b, 2)
    return (jnp.dot(a_hi, b_hi, preferred_element_type=F32)
            + jnp.dot(a_hi, b_lo, preferred_element_type=F32)
            + jnp.dot(a_lo, b_hi, preferred_element_type=F32))


def _rms(x, g):
    return x * lax.rsqrt(jnp.mean(x * x, axis=-1, keepdims=True) + NORM_EPS) * g


def _sigmoid(x):
    return 1.0 / (1.0 + jnp.exp(-x))


def _silu(x):
    return x * _sigmoid(x)


def _row_tile(t, target):
    tm = min(t, target)
    while t % tm:
        tm //= 2
    return tm


def _const_spec(shape):
    nd = len(shape)
    return pl.BlockSpec(shape, lambda *_: (0,) * nd)


def _rwkv_in_kernel(tiles_per_seq, has_vres, x_ref, xp_ref, g_ref, mu_ref, w4_ref, ws_ref,
                    wup_ref, w0_ref, aup_ref, a0_ref, *rest):
    if has_vres:
        wv_ref, vup_ref, v0_ref, vf_ref, r_o, w_o, k_o, v_o, a_o, g_o = rest
    else:
        r_o, w_o, k_o, v_o, a_o, g_o = rest
    width = r_o.shape[-1]
    g = g_ref[...]
    h = _rms(x_ref[...], g)
    hp = _rms(xp_ref[...], g)[7:8, :]
    first = (pl.program_id(0) % tiles_per_seq) == 0
    hp = jnp.where(first, 0.0, hp)
    rows = lax.broadcasted_iota(jnp.int32, h.shape, 0)
    h_prev = jnp.where(rows == 0, hp, pltpu.roll(h, 1, 0))
    dx = h_prev - h
    mu = mu_ref[...]

    def mix(i):
        return (h + dx * mu[i:i + 1, :]).astype(BF16)

    w_lo = jnp.tanh(jnp.dot(mix(1), ws_ref[...], preferred_element_type=F32))
    a_lo = jnp.dot(mix(4), ws_ref[...], preferred_element_type=F32)
    w_raw = w0_ref[...] + _dot(w_lo, wup_ref[...])
    w_o[...] = -_sigmoid(w_raw) * math.exp(-0.5)
    a_o[...] = _sigmoid(a0_ref[...] + _dot(a_lo, aup_ref[...]))
    g_o[...] = _silu(jnp.dot(mix(5), w4_ref[:, 3 * width:4 * width], preferred_element_type=F32))
    x_v = mix(3)
    v = jnp.dot(x_v, w4_ref[:, 2 * width:3 * width], preferred_element_type=F32)
    if has_vres:
        v_lo = jnp.dot(x_v, wv_ref[...], preferred_element_type=F32)
        gate = _sigmoid(v0_ref[...] + _dot(v_lo, vup_ref[...]))
        v = v + (vf_ref[...] - v) * gate
    v_o[...] = v
    r_o[...] = jnp.dot(mix(0), w4_ref[:, 0:width], preferred_element_type=F32)
    k_o[...] = jnp.dot(mix(2), w4_ref[:, width:2 * width], preferred_element_type=F32)


def _rwkv_in(x2d, seq, norm_g, mu, w_in, w_up, w0, a_up, a0, v_up=None, v0=None, v_first=None):
    n, d = x2d.shape
    width = w_up.shape[1]
    rank_w, rank_a = w_up.shape[0], a_up.shape[0]
    has_vres = v_up is not None
    tm = _row_tile(seq, 256)
    tiles_per_seq = seq // tm
    w4 = w_in[:, :4 * width].astype(BF16)
    ws = w_in[:, 4 * width:4 * width + rank_w + rank_a].astype(BF16)
    wup = jnp.concatenate([w_up, jnp.zeros((rank_a, width), F32)], 0).astype(BF16)
    aup = jnp.concatenate([jnp.zeros((rank_w, width), F32), a_up], 0).astype(BF16)
    mu8 = jnp.concatenate([mu, jnp.zeros((2, d), F32)], 0)
    row = lambda p: p.reshape(1, -1)
    tile = pl.BlockSpec((tm, d), lambda i: (i, 0))
    wtile = pl.BlockSpec((tm, width), lambda i: (i, 0))
    prev = pl.BlockSpec((8, d), lambda i: (jnp.maximum(i * (tm // 8) - 1, 0), 0))
    args = [x2d, x2d, row(norm_g), mu8, w4, ws, wup, row(w0), aup, row(a0)]
    specs = [tile, prev, _const_spec((1, d)), _const_spec((8, d)), _const_spec(w4.shape),
             _const_spec(ws.shape), _const_spec(wup.shape), _const_spec((1, width)),
             _const_spec(aup.shape), _const_spec((1, width))]
    if has_vres:
        rank_v = v_up.shape[0]
        wv = jnp.pad(w_in[:, 4 * width + rank_w + rank_a:], ((0, 0), (0, LANES - rank_v))).astype(BF16)
        vup = jnp.pad(v_up, ((0, LANES - rank_v), (0, 0))).astype(BF16)
        args += [wv, vup, row(v0), v_first]
        specs += [_const_spec(wv.shape), _const_spec(vup.shape), _const_spec((1, width)), wtile]
    out_shape = [jax.ShapeDtypeStruct((n, width), F32)] * 6
    return pl.pallas_call(
        functools.partial(_rwkv_in_kernel, tiles_per_seq, has_vres),
        grid=(n // tm,),
        in_specs=specs,
        out_specs=[wtile] * 6,
        out_shape=out_shape,
        compiler_params=_cparams("parallel"),
        name="rwkv_in",
    )(*args)


def _rwkv_scan_kernel(r_ref, w_ref, k_ref, v_ref, a_ref, g_ref, kk_p, ka_p, rk_p, gw_p, gb_p,
                      y_ref, st_ref):
    L = RW_CHUNK
    tb = r_ref.shape[1]
    tiles = r_ref.shape[2] // LANES
    n2 = 2 * L
    rows = lax.broadcasted_iota(jnp.int32, (n2, LANES), 0)
    cols = lax.broadcasted_iota(jnp.int32, (n2, LANES), 1)
    own = (rows < L) == (cols < HEAD)
    lower = rows > cols
    lower_eq = rows >= cols
    eye = (rows == cols).astype(F32)
    ri = lax.broadcasted_iota(jnp.int32, (L, L), 0)
    ci = lax.broadcasted_iota(jnp.int32, (L, L), 1)
    tril_incl = (ri >= ci).astype(BF16)

    def dup(x):
        return jnp.concatenate([x, x], axis=0)

    def stack(x):
        return jnp.where(own, dup(x), 0.0)

    @pl.when(pl.program_id(2) == 0)
    def _():
        st_ref[...] = jnp.zeros_like(st_ref)

    def one_tile(bi, p, sl, cw, lw):
        ls = slice(p * LANES, (p + 1) * LANES)
        k_k, k_a, r_k = kk_p[:, ls], ka_p[:, ls], rk_p[:, ls]
        r = r_ref[bi, sl, ls]
        k = k_ref[bi, sl, ls]
        v = v_ref[bi, sl, ls]
        a = a_ref[bi, sl, ls]
        cw_last = cw[L - 1:L, :]
        e_in = jnp.exp(cw)
        e_ex = jnp.exp(cw - lw)
        e_inv = jnp.exp(-cw)
        e_end = jnp.exp(cw_last - cw)
        kk_bd = stack(k * k_k)
        nrm = jnp.sqrt(jnp.sum(kk_bd * kk_bd, axis=-1, keepdims=True))
        kk_bd = kk_bd * (1.0 / jnp.maximum(nrm, 1e-12))
        k_mod = k * (1.0 + (a - 1.0) * k_a)
        at_bd = (-kk_bd * dup(e_ex)).astype(BF16)
        bt_bd = (kk_bd * dup(a * e_inv)).astype(BF16)
        kt_bd = stack(k_mod * e_inv).astype(BF16)
        rt_bd = stack(r * e_in).astype(BF16)
        v_bd = stack(v)
        v_bf = v_bd.astype(BF16)
        bw_bd = (kk_bd * dup(a * e_end)).astype(BF16)
        kw_bd = stack(k_mod * e_end).astype(BF16)

        m = _dot_nt(jnp.concatenate([at_bd, rt_bd], axis=0),
                    jnp.concatenate([bt_bd, kt_bd], axis=0))
        yield
        a_ab = jnp.where(lower, m[:n2, :n2], 0.0)
        a_ak = jnp.where(lower, m[:n2, n2:], 0.0).astype(BF16)
        a_r = jnp.where(jnp.concatenate([lower_eq, lower_eq], axis=1), m[n2:, :], 0.0).astype(BF16)

        x = eye + a_ab
        pw = a_ab.astype(BF16)
        pw = _dot(pw, pw)
        yield
        pw = pw.astype(BF16)
        steps = int(math.log2(L)) - 1
        for s in range(steps):
            if s + 1 < steps:
                px = _dot(pw, jnp.concatenate([pw, x.astype(BF16)], axis=1))
                yield
                pw = px[:, :n2].astype(BF16)
                x = x + px[:, n2:]
            else:
                px = _dot(pw, x)
                yield
                x = x + px

        st = st_ref[bi * tiles + p]
        rhs = _dot_nt(at_bd, st) + _dot(a_ak, v_bf)
        y0 = _dot_nt(rt_bd, st)
        yield
        u = _dot(x, rhs)
        yield
        uv = jnp.concatenate([u.astype(BF16), v_bf], axis=0)
        y = y0 + _dot(a_r, uv)
        st_ref[bi * tiles + p] = st * jnp.exp(cw_last) + _dot_tn(uv, jnp.concatenate([bw_bd, kw_bd], axis=0))
        yield

        gw_bd = stack(jnp.broadcast_to(gw_p[:, ls], (L, LANES)))
        gb_bd = stack(jnp.broadcast_to(gb_p[:, ls], (L, LANES)))
        mean = jnp.sum(y, axis=-1, keepdims=True) * (1.0 / HEAD)
        d = jnp.where(own, y - mean, 0.0)
        var = jnp.sum(d * d, axis=-1, keepdims=True) * (1.0 / HEAD)
        yn = d * lax.rsqrt(var + RW_GN_EPS) * gw_bd + gb_bd
        bonus = jnp.sum(stack(r * k_mod * r_k), axis=-1, keepdims=True)
        yo = yn + bonus * v_bd
        y_ref[bi, sl, ls] = ((yo[:L] + yo[L:]) * g_ref[bi, sl, ls]).astype(BF16)

    def chunk(c, carry):
        sl = pl.ds(pl.multiple_of(c * L, L), L)
        running = []
        for bi in range(r_ref.shape[0]):
            lw = w_ref[bi, sl, :]
            cw = _dot_sel_l(tril_incl, lw)
            running += [one_tile(bi, p, sl, cw[:, p * LANES:(p + 1) * LANES],
                                 lw[:, p * LANES:(p + 1) * LANES]) for p in range(tiles)]
        while running:
            running = [t for t in running if next(t, "done") != "done"]
        return carry

    lax.fori_loop(0, tb // L, chunk, 0)


def _rwkv_scan(r, w, k, v, a, g, k_k, k_a, r_k, gn_w, gn_b):
    b, seq, width = r.shape
    gl = RW_TILES * LANES
    tb = _row_tile(seq, RW_TIME_BLOCK)
    nb = RW_BATCH if b % RW_BATCH == 0 else 1
    blk = pl.BlockSpec((nb, tb, gl), lambda i, j, t: (i, t, j))
    par = pl.BlockSpec((1, gl), lambda i, j, t: (0, j))
    row = lambda p: p.reshape(1, -1)
    return pl.pallas_call(
        _rwkv_scan_kernel,
        grid=(b // nb, width // gl, seq // tb),
        in_specs=[blk] * 6 + [par] * 5,
        out_specs=blk,
        out_shape=jax.ShapeDtypeStruct((b, seq, width), BF16),
        scratch_shapes=[pltpu.VMEM((nb * RW_TILES, LANES, LANES), F32)],
        compiler_params=_cparams("parallel", "parallel", "arbitrary"),
        name="rwkv_scan",
    )(r, w, k, v, a, g, row(k_k), row(k_a), row(r_k), row(gn_w), row(gn_b))


def _out_proj_kernel(x_ref, yg_ref, w_ref, o_ref):
    o_ref[...] = x_ref[...] + jnp.dot(yg_ref[...], w_ref[...], preferred_element_type=F32)


def _out_proj(x2d, yg2d, w_out, seq):
    n, d = x2d.shape
    width = yg2d.shape[1]
    tm = _row_tile(seq, 512)
    return pl.pallas_call(
        _out_proj_kernel,
        grid=(n // tm,),
        in_specs=[pl.BlockSpec((tm, d), lambda i: (i, 0)),
                  pl.BlockSpec((tm, width), lambda i: (i, 0)),
                  _const_spec((width, d))],
        out_specs=pl.BlockSpec((tm, d), lambda i: (i, 0)),
        out_shape=jax.ShapeDtypeStruct((n, d), F32),
        compiler_params=_cparams("parallel"),
        name="out_proj",
    )(x2d, yg2d, w_out.astype(BF16))


def _rwkv_layer(x, norm_g, p, v_first):
    mu, w_in, w_up, w0, a_up, a0, k_k, k_a, r_k, gn_w, gn_b, w_out = p[:12]
    b, seq, d = x.shape
    x2d = x.reshape(b * seq, d)
    if v_first is None:
        r, w, k, v, a, g = _rwkv_in(x2d, seq, norm_g, mu, w_in, w_up, w0, a_up, a0)
        v_first = v
    else:
        v_up, v0 = p[12:]
        r, w, k, v, a, g = _rwkv_in(x2d, seq, norm_g, mu, w_in, w_up, w0, a_up, a0,
                                    v_up, v0, v_first)
    width = r.shape[1]
    sh = lambda t: t.reshape(b, seq, width)
    yg = _rwkv_scan(sh(r), sh(w), sh(k), sh(v), sh(a), sh(g), k_k, k_a, r_k.reshape(-1),
                    gn_w, gn_b)
    out = _out_proj(x2d, yg.reshape(b * seq, width), w_out, seq)
    return out.reshape(b, seq, d), v_first


def _norm_proj_kernel(acts, x_ref, g_ref, *refs):
    n = len(acts)
    w_refs, o_refs = refs[:n], refs[n:]
    h = _rms(x_ref[...], g_ref[...]).astype(BF16)
    for i in sorted(range(n), key=lambda i: not acts[i]):
        y = jnp.dot(h, w_refs[i][...], preferred_element_type=F32)
        o_refs[i][...] = _silu(y) if acts[i] else y


def _norm_proj(x2d, seq, norm_g, weights, acts):
    n, d = x2d.shape
    tm = _row_tile(seq, 256)
    weights = [w.astype(BF16) for w in weights]
    return pl.pallas_call(
        functools.partial(_norm_proj_kernel, tuple(acts)),
        grid=(n // tm,),
        in_specs=[pl.BlockSpec((tm, d), lambda i: (i, 0)), _const_spec((1, d))]
                 + [_const_spec(w.shape) for w in weights],
        out_specs=[pl.BlockSpec((tm, w.shape[1]), lambda i: (i, 0)) for w in weights],
        out_shape=[jax.ShapeDtypeStruct((n, w.shape[1]), F32) for w in weights],
        compiler_params=_cparams("parallel"),
        name="norm_proj",
    )(x2d, norm_g.reshape(1, d), *weights)


def _ssd_kernel(x_ref, z_ref, u_ref, dt_ref, cw_ref, cb_ref, dtb_ref, alog_ref, dsk_ref,
                gnw_ref, exp_ref, wout_ref, o_ref, ext_ref, h_ref, y_ref):
    L = SSD_CHUNK
    width = z_ref.shape[-1]
    heads_per_group = width // HEAD // SSD_GROUPS
    gw = heads_per_group * HEAD
    c = pl.program_id(1)

    @pl.when(c == 0)
    def _():
        ext_ref[0:8, :] = jnp.zeros((8, ext_ref.shape[1]), F32)
        h_ref[...] = jnp.zeros_like(h_ref)

    u = u_ref[0]
    ext_ref[8:8 + L, :] = u
    conv = cb_ref[...] + cw_ref[SSD_CONV - 1:SSD_CONV, :] * u
    for j in range(SSD_CONV - 1):
        conv = conv + cw_ref[j:j + 1, :] * ext_ref[pl.ds(8 - (SSD_CONV - 1) + j, L), :]
    ext_ref[0:8, :] = u[L - 8:, :]
    xbc = _silu(conv)
    xs = xbc[:, :width]
    n_bc = SSD_GROUPS * SSD_STATE
    bm = xbc[:, width:width + n_bc]
    cm = xbc[:, width + n_bc:]

    x_in = dt_ref[0] + dtb_ref[...]
    dt = jnp.maximum(x_in, 0.0) + jnp.log1p(jnp.exp(-jnp.abs(x_in)))
    a = dt * (-jnp.exp(alog_ref[...]))
    ri = lax.broadcasted_iota(jnp.int32, (L, L), 0)
    ci = lax.broadcasted_iota(jnp.int32, (L, L), 1)
    causal = ri >= ci
    ac = _dot_sel_l(causal.astype(BF16), a)
    ac_t = ac.T
    expand = exp_ref[...]
    dt_w = _dot_sel_r(dt, expand)
    ac_w = _dot_sel_r(ac, expand)
    ac_last_w = ac_w[L - 1:L, :]
    x_dt = xs * dt_w
    e_out = jnp.exp(ac_w)
    xd = x_dt * jnp.exp(ac_last_w - ac_w)
    e_last = jnp.exp(ac_last_w)
    lane_g = lax.broadcasted_iota(jnp.int32, (L, gw), 1) // HEAD

    for g in range(SSD_GROUPS):
        bg = bm[:, g * SSD_STATE:(g + 1) * SSD_STATE]
        cg = cm[:, g * SSD_STATE:(g + 1) * SSD_STATE]
        gs = slice(g * gw, (g + 1) * gw)
        cb = _dot_nt(cg, bg)
        h_prev = h_ref[g]
        y_off = _dot(cg, h_prev) * e_out[:, gs]
        x_g = x_dt[:, gs].astype(BF16)
        lms, xms = [], []
        for r in range(heads_per_group):
            hd = g * heads_per_group + r
            seg = ac[:, hd:hd + 1] - ac_t[hd:hd + 1, :]
            lms.append((cb * jnp.exp(jnp.where(causal, seg, -1e30))).astype(BF16))
            xms.append(jnp.where(lane_g == r, x_g, jnp.zeros_like(x_g)))
        y_diag = jnp.dot(jnp.concatenate(lms, axis=1), jnp.concatenate(xms, axis=0),
                         preferred_element_type=F32)
        h_ref[g] = h_prev * e_last[:, gs] + _dot_tn(bg, xd[:, gs])
        y_ref[:, gs] = y_off + y_diag

    y = (y_ref[...] + xs * dsk_ref[...]) * _silu(z_ref[0])
    for g in range(SSD_GROUPS):
        gs = slice(g * gw, (g + 1) * gw)
        yg = y[:, gs]
        ms = jnp.mean(yg * yg, axis=-1, keepdims=True)
        y_ref[:, gs] = yg * lax.rsqrt(ms + NORM_EPS)
    yn = (y_ref[...] * gnw_ref[...]).astype(BF16)
    o_ref[0] = x_ref[0] + jnp.dot(yn, wout_ref[...], preferred_element_type=F32)


def _mamba_layer(x, norm_g, p):
    w_in, conv_w, conv_b, dt_bias, a_log, d_skip, gnorm_w, w_out = p
    b, seq, d = x.shape
    width = w_out.shape[0]
    heads = d_skip.shape[0]
    conv_dim = conv_w.shape[1]
    L = SSD_CHUNK
    x2d = x.reshape(b * seq, d)
    w_dt = jnp.pad(w_in[:, width + conv_dim:], ((0, 0), (0, LANES - heads)))
    z, u, dt = _norm_proj(x2d, seq, norm_g,
                          [w_in[:, :width], w_in[:, width:width + conv_dim], w_dt],
                          [False, False, False])
    pad_h = lambda v: jnp.pad(v, (0, LANES - heads)).reshape(1, LANES)
    expand = (jnp.arange(LANES)[:, None] == (jnp.arange(width) // HEAD)[None, :]).astype(BF16)
    cw8 = jnp.pad(conv_w, ((0, 8 - conv_w.shape[0]), (0, 0)))
    gw = width // SSD_GROUPS
    blk = lambda n: pl.BlockSpec((1, L, n), lambda i, j: (i, j, 0))
    out = pl.pallas_call(
        _ssd_kernel,
        grid=(b, seq // L),
        in_specs=[blk(d), blk(width), blk(conv_dim), blk(LANES),
                  _const_spec((8, conv_dim)), _const_spec((1, conv_dim)),
                  _const_spec((1, LANES)), _const_spec((1, LANES)),
                  _const_spec((1, width)), _const_spec((1, width)),
                  _const_spec((LANES, width)), _const_spec((width, d))],
        out_specs=blk(d),
        out_shape=jax.ShapeDtypeStruct((b, seq, d), F32),
        scratch_shapes=[pltpu.VMEM((8 + L, conv_dim), F32),
                        pltpu.VMEM((SSD_GROUPS, SSD_STATE, gw), F32),
                        pltpu.VMEM((L, width), F32)],
        compiler_params=_cparams("parallel", "arbitrary"),
        name="ssd",
    )(x, z.reshape(b, seq, width), u.reshape(b, seq, conv_dim), dt.reshape(b, seq, LANES),
      cw8, conv_b.reshape(1, -1), pad_h(dt_bias), pad_h(a_log),
      jnp.repeat(d_skip, HEAD).reshape(1, -1), gnorm_w.reshape(1, -1), expand,
      w_out.astype(BF16))
    return out


def _sb_kernel(q_ref, k_ref, v_ref, g_ref, qn_ref, kn_ref, o_ref, ks_ref, vs_ref):
    blk = q_ref.shape[1]
    seq = k_ref.shape[1]
    i = pl.program_id(2)
    lane = lax.broadcasted_iota(jnp.int32, (blk, LANES), 1)
    head0 = lane < HEAD

    def head_rms(t, g):
        sq = t * t
        s0 = jnp.sum(jnp.where(head0, sq, 0.0), axis=-1, keepdims=True)
        s1 = jnp.sum(jnp.where(head0, 0.0, sq), axis=-1, keepdims=True)
        inv = jnp.where(head0, lax.rsqrt(s0 * (1.0 / HEAD) + NORM_EPS),
                        lax.rsqrt(s1 * (1.0 / HEAD) + NORM_EPS))
        return t * inv * g

    @pl.when(i == 0)
    def _():
        def norm_block(j, carry):
            sl = pl.ds(pl.multiple_of(j * blk, blk), blk)
            ks_ref[sl, :] = head_rms(k_ref[0, sl, :], kn_ref[...]).astype(BF16)
            vs_ref[sl, :] = v_ref[0, sl, :].astype(BF16)
            return carry
        lax.fori_loop(0, seq // blk, norm_block, 0)

    q = head_rms(q_ref[0], qn_ref[...]) * (1.0 / math.sqrt(HEAD))
    q_bd = jnp.concatenate([jnp.where(head0, q, 0.0), jnp.where(head0, 0.0, q)],
                           axis=0).astype(BF16)
    gr = min(SB_ROWS, blk)
    ngroups = 2 * blk // gr
    ji = lax.broadcasted_iota(jnp.int32, (blk, blk), 0)
    si = lax.broadcasted_iota(jnp.int32, (blk, blk), 1)
    suffix = (ji > si).astype(BF16)

    def group(g, j, get_run, diagonal, out, present=None):
        first = (g * gr) % blk
        nk = first + gr if diagonal else blk
        sl = pl.ds(pl.multiple_of(j * blk, blk), nk)
        z = _dot_nt(q_bd[g * gr:(g + 1) * gr], ks_ref[sl, :])
        yield
        log_beta = jnp.minimum(z, 0.0) - jnp.log(1.0 + jnp.exp(-jnp.abs(z)))
        log_1m = log_beta - z
        if diagonal:
            strict = (lax.broadcasted_iota(jnp.int32, (gr, nk), 0) + first
                      > lax.broadcasted_iota(jnp.int32, (gr, nk), 1))
            log_1m = jnp.where(strict, log_1m, 0.0)
        tail = _dot_sel_r(log_1m, suffix[:nk, :nk], terms=2)
        total = jnp.sum(log_1m, axis=-1, keepdims=True)
        yield
        run = get_run()
        run_in = run if present is None else jnp.where(present, run, -1e30)
        att = jnp.exp(log_beta + tail + run_in)
        if diagonal:
            att = jnp.where(strict, att, 0.0)
        if present is not None:
            total = jnp.where(present, total, 0.0)
        out[g] = (run + total, _dot(att, vs_ref[sl, :]))
        yield

    def lockstep(gens):
        while gens:
            gens = [t for t in gens if next(t, "done") != "done"]

    def live(runs):
        top = functools.reduce(jnp.maximum, [jnp.max(r) for r in runs])
        return (top > -SB_DEAD).astype(jnp.int32)

    zero = jnp.zeros((gr, 1), F32)
    d_out, p_out = {}, {}
    lockstep([group(g, i, lambda: zero, True, d_out) for g in range(ngroups)]
             + [group(g, jnp.maximum(i - 1, 0), lambda g=g: d_out[g][0], False, p_out, i > 0)
                for g in range(ngroups)])
    runs = [p_out[g][0] for g in range(ngroups)]
    accs = [d_out[g][1] + p_out[g][1] for g in range(ngroups)]

    def cond(c):
        jj, _, _, go = c
        return jnp.logical_and(jj <= i, go > 0)

    def body(c):
        jj, runs, accs, _ = c
        out = {}
        lockstep([group(g, i - jj, lambda g=g: runs[g], False, out) for g in range(ngroups)])
        runs = [out[g][0] for g in range(ngroups)]
        return jj + 1, runs, [accs[g] + out[g][1] for g in range(ngroups)], live(runs)

    _, _, accs, _ = lax.while_loop(cond, body, (jnp.int32(2), runs, accs, live(runs)))
    acc = jnp.concatenate(accs, axis=0)
    o_ref[0] = (jnp.where(head0, acc[:blk], acc[blk:]) * g_ref[0]).astype(BF16)


def _sb_layer(x, norm_g, p):
    w_in, q_norm, k_norm, w_out = p
    b, seq, d = x.shape
    width = w_out.shape[0]
    x2d = x.reshape(b * seq, d)
    q, k, v, g = _norm_proj(x2d, seq, norm_g, [w_in[:, i * width:(i + 1) * width] for i in range(4)],
                            [False, False, False, True])
    sh = lambda t: t.reshape(b, seq, width)
    pair = lambda t: jnp.tile(t, LANES // HEAD).reshape(1, LANES)
    full = pl.BlockSpec((1, seq, LANES), lambda bi, pi, qi: (bi, 0, pi))
    blk = _row_tile(seq, SB_BLOCK)
    qblk = pl.BlockSpec((1, blk, LANES), lambda bi, pi, qi: (bi, qi, pi))
    o = pl.pallas_call(
        _sb_kernel,
        grid=(b, width // LANES, seq // blk),
        in_specs=[qblk, full, full, qblk, _const_spec((1, LANES)), _const_spec((1, LANES))],
        out_specs=qblk,
        out_shape=jax.ShapeDtypeStruct((b, seq, width), BF16),
        scratch_shapes=[pltpu.VMEM((seq, LANES), BF16), pltpu.VMEM((seq, LANES), BF16)],
        compiler_params=_cparams("parallel", "parallel", "arbitrary"),
        name="sb_attn",
    )(sh(q), sh(k), sh(v), sh(g), pair(q_norm), pair(k_norm))
    out = _out_proj(x2d, o.reshape(b * seq, width), w_out, seq)
    return out.reshape(b, seq, d)


def kernel(x, l0_norm, l0_mu, l0_w_in, l0_w_up, l0_w0, l0_a_up, l0_a0, l0_k_k, l0_k_a, l0_r_k, l0_gn_w, l0_gn_b, l0_w_out, l1_norm, l1_w_in, l1_conv_w, l1_conv_b, l1_dt_bias, l1_a_log, l1_d_skip, l1_gnorm_w, l1_w_out, l2_norm, l2_w_in, l2_q_norm, l2_k_norm, l2_w_out, l3_norm, l3_mu, l3_w_in, l3_w_up, l3_w0, l3_a_up, l3_a0, l3_k_k, l3_k_a, l3_r_k, l3_gn_w, l3_gn_b, l3_w_out, l3_v_up, l3_v0):
    p0 = (l0_mu, l0_w_in, l0_w_up, l0_w0, l0_a_up, l0_a0, l0_k_k, l0_k_a, l0_r_k,
          l0_gn_w, l0_gn_b, l0_w_out)
    p3 = (l3_mu, l3_w_in, l3_w_up, l3_w0, l3_a_up, l3_a0, l3_k_k, l3_k_a, l3_r_k,
          l3_gn_w, l3_gn_b, l3_w_out, l3_v_up, l3_v0)
    p1 = (l1_w_in, l1_conv_w, l1_conv_b, l1_dt_bias, l1_a_log, l1_d_skip, l1_gnorm_w, l1_w_out)
    p2 = (l2_w_in, l2_q_norm, l2_k_norm, l2_w_out)
    x, v_first = _rwkv_layer(x, l0_norm, p0, None)
    x = _mamba_layer(x, l1_norm, p1)
    x = _sb_layer(x, l2_norm, p2)
    x, _ = _rwkv_layer(x, l3_norm, p3, v_first)
    return x
```

```python
import functools
import math

import jax
import jax.numpy as jnp
from jax import lax
from jax.experimental import pallas as pl
from jax.experimental.pallas import tpu as pltpu

F32 = jnp.float32
BF16 = jnp.bfloat16

LANES = 128
HEAD = 64
NORM_EPS = 1e-6
RW_GN_EPS = 64e-5
RW_CHUNK = 64
RW_TILES = 8
RW_BATCH = 4
RW_TIME_BLOCK = 128
SSD_CHUNK = 128
SSD_STATE = 128
SSD_GROUPS = 8
SSD_CONV = 4
SB_BLOCK = 256
SB_ROWS = 128
SB_DEAD = 104.0
VMEM_LIMIT = 56 * 1024 * 1024


def _cparams(*sem):
    return pltpu.CompilerParams(dimension_semantics=sem, vmem_limit_bytes=VMEM_LIMIT)


def _dot(a, b):
    return jnp.dot(a.astype(BF16), b.astype(BF16), preferred_element_type=F32)


def _dot_nt(a, b):
    return lax.dot_general(a.astype(BF16), b.astype(BF16), (((1,), (1,)), ((), ())),
                           preferred_element_type=F32)


def _dot_tn(a, b):
    return lax.dot_general(a.astype(BF16), b.astype(BF16), (((0,), (0,)), ((), ())),
                           preferred_element_type=F32)


def _split(x, terms):
    out = []
    rem = x
    for i in range(terms):
        p = rem.astype(BF16)
        out.append(p)
        if i + 1 < terms:
            rem = rem - p.astype(F32)
    return out


def _dot_sel_l(sel_bf16, x, terms=3):
    acc = None
    for p in _split(x, terms):
        d = jnp.dot(sel_bf16, p, preferred_element_type=F32)
        acc = d if acc is None else acc + d
    return acc


def _dot_sel_r(x, sel_bf16, terms=3):
    acc = None
    for p in _split(x, terms):
        d = jnp.dot(p, sel_bf16, preferred_element_type=F32)
        acc = d if acc is None else acc + d
    return acc


def _dot3(a, b):
    a_hi, a_lo = _split(a, 2)
    b_hi, b_lo = _split(b, 2)
    return (jnp.dot(a_hi, b_hi, preferred_element_type=F32)
            + jnp.dot(a_hi, b_lo, preferred_element_type=F32)
            + jnp.dot(a_lo, b_hi, preferred_element_type=F32))


def _rms(x, g):
    return x * lax.rsqrt(jnp.mean(x * x, axis=-1, keepdims=True) + NORM_EPS) * g


def _sigmoid(x):
    return 1.0 / (1.0 + jnp.exp(-x))


def _silu(x):
    return x * _sigmoid(x)


def _row_tile(t, target):
    tm = min(t, target)
    while t % tm:
        tm //= 2
    return tm


def _const_spec(shape):
    nd = len(shape)
    return pl.BlockSpec(shape, lambda *_: (0,) * nd)


def _rwkv_in_kernel(tiles_per_seq, has_vres, x_ref, xp_ref, g_ref, mu_ref, w4_ref, ws_ref,
                    wup_ref, w0_ref, aup_ref, a0_ref, *rest):
    if has_vres:
        wv_ref, vup_ref, v0_ref, vf_ref, r_o, w_o, k_o, v_o, a_o, g_o = rest
    else:
        r_o, w_o, k_o, v_o, a_o, g_o = rest
    width = r_o.shape[-1]
    g = g_ref[...]
    h = _rms(x_ref[...], g)
    hp = _rms(xp_ref[...], g)[7:8, :]
    first = (pl.program_id(0) % tiles_per_seq) == 0
    hp = jnp.where(first, 0.0, hp)
    rows = lax.broadcasted_iota(jnp.int32, h.shape, 0)
    h_prev = jnp.where(rows == 0, hp, pltpu.roll(h, 1, 0))
    dx = h_prev - h
    mu = mu_ref[...]

    def mix(i):
        return (h + dx * mu[i:i + 1, :]).astype(BF16)

    w_lo = jnp.tanh(jnp.dot(mix(1), ws_ref[...], preferred_element_type=F32))
    a_lo = jnp.dot(mix(4), ws_ref[...], preferred_element_type=F32)
    w_raw = w0_ref[...] + _dot(w_lo, wup_ref[...])
    w_o[...] = -_sigmoid(w_raw) * math.exp(-0.5)
    a_o[...] = _sigmoid(a0_ref[...] + _dot(a_lo, aup_ref[...]))
    g_o[...] = _silu(jnp.dot(mix(5), w4_ref[:, 3 * width:4 * width], preferred_element_type=F32))
    x_v = mix(3)
    v = jnp.dot(x_v, w4_ref[:, 2 * width:3 * width], preferred_element_type=F32)
    if has_vres:
        v_lo = jnp.dot(x_v, wv_ref[...], preferred_element_type=F32)
        gate = _sigmoid(v0_ref[...] + _dot(v_lo, vup_ref[...]))
        v = v + (vf_ref[...] - v) * gate
    v_o[...] = v
    r_o[...] = jnp.dot(mix(0), w4_ref[:, 0:width], preferred_element_type=F32)
    k_o[...] = jnp.dot(mix(2), w4_ref[:, width:2 * width], preferred_element_type=F32)


def _rwkv_in(x2d, seq, norm_g, mu, w_in, w_up, w0, a_up, a0, v_up=None, v0=None, v_first=None):
    n, d = x2d.shape
    width = w_up.shape[1]
    rank_w, rank_a = w_up.shape[0], a_up.shape[0]
    has_vres = v_up is not None
    tm = _row_tile(seq, 256)
    tiles_per_seq = seq // tm
    w4 = w_in[:, :4 * width].astype(BF16)
    ws = w_in[:, 4 * width:4 * width + rank_w + rank_a].astype(BF16)
    wup = jnp.concatenate([w_up, jnp.zeros((rank_a, width), F32)], 0).astype(BF16)
    aup = jnp.concatenate([jnp.zeros((rank_w, width), F32), a_up], 0).astype(BF16)
    mu8 = jnp.concatenate([mu, jnp.zeros((2, d), F32)], 0)
    row = lambda p: p.reshape(1, -1)
    tile = pl.BlockSpec((tm, d), lambda i: (i, 0))
    wtile = pl.BlockSpec((tm, width), lambda i: (i, 0))
    prev = pl.BlockSpec((8, d), lambda i: (jnp.maximum(i * (tm // 8) - 1, 0), 0))
    args = [x2d, x2d, row(norm_g), mu8, w4, ws, wup, row(w0), aup, row(a0)]
    specs = [tile, prev, _const_spec((1, d)), _const_spec((8, d)), _const_spec(w4.shape),
             _const_spec(ws.shape), _const_spec(wup.shape), _const_spec((1, width)),
             _const_spec(aup.shape), _const_spec((1, width))]
    if has_vres:
        rank_v = v_up.shape[0]
        wv = jnp.pad(w_in[:, 4 * width + rank_w + rank_a:], ((0, 0), (0, LANES - rank_v))).astype(BF16)
        vup = jnp.pad(v_up, ((0, LANES - rank_v), (0, 0))).astype(BF16)
        args += [wv, vup, row(v0), v_first]
        specs += [_const_spec(wv.shape), _const_spec(vup.shape), _const_spec((1, width)), wtile]
    out_shape = [jax.ShapeDtypeStruct((n, width), F32)] * 6
    return pl.pallas_call(
        functools.partial(_rwkv_in_kernel, tiles_per_seq, has_vres),
        grid=(n // tm,),
        in_specs=specs,
        out_specs=[wtile] * 6,
        out_shape=out_shape,
        compiler_params=_cparams("parallel"),
        name="rwkv_in",
    )(*args)


def _rwkv_scan_kernel(r_ref, w_ref, k_ref, v_ref, a_ref, g_ref, kk_p, ka_p, rk_p, gw_p, gb_p,
                      y_ref, st_ref):
    L = RW_CHUNK
    tb = r_ref.shape[1]
    tiles = r_ref.shape[2] // LANES
    n2 = 2 * L
    rows = lax.broadcasted_iota(jnp.int32, (n2, LANES), 0)
    cols = lax.broadcasted_iota(jnp.int32, (n2, LANES), 1)
    own = (rows < L) == (cols < HEAD)
    lower = rows > cols
    lower_eq = rows >= cols
    eye = (rows == cols).astype(F32)
    ri = lax.broadcasted_iota(jnp.int32, (L, L), 0)
    ci = lax.broadcasted_iota(jnp.int32, (L, L), 1)
    tril_incl = (ri >= ci).astype(BF16)

    def dup(x):
        return jnp.concatenate([x, x], axis=0)

    def stack(x):
        return jnp.where(own, dup(x), 0.0)

    @pl.when(pl.program_id(2) == 0)
    def _():
        st_ref[...] = jnp.zeros_like(st_ref)

    def one_tile(bi, p, sl, cw, lw):
        ls = slice(p * LANES, (p + 1) * LANES)
        k_k, k_a, r_k = kk_p[:, ls], ka_p[:, ls], rk_p[:, ls]
        r = r_ref[bi, sl, ls]
        k = k_ref[bi, sl, ls]
        v = v_ref[bi, sl, ls]
        a = a_ref[bi, sl, ls]
        cw_last = cw[L - 1:L, :]
        e_in = jnp.exp(cw)
        e_ex = jnp.exp(cw - lw)
        e_inv = jnp.exp(-cw)
        e_end = jnp.exp(cw_last - cw)
        kk_bd = stack(k * k_k)
        nrm = jnp.sqrt(jnp.sum(kk_bd * kk_bd, axis=-1, keepdims=True))
        kk_bd = kk_bd * (1.0 / jnp.maximum(nrm, 1e-12))
        k_mod = k * (1.0 + (a - 1.0) * k_a)
        at_bd = (-kk_bd * dup(e_ex)).astype(BF16)
        bt_bd = (kk_bd * dup(a * e_inv)).astype(BF16)
        kt_bd = stack(k_mod * e_inv).astype(BF16)
        rt_bd = stack(r * e_in).astype(BF16)
        v_bd = stack(v)
        v_bf = v_bd.astype(BF16)
        bw_bd = (kk_bd * dup(a * e_end)).astype(BF16)
        kw_bd = stack(k_mod * e_end).astype(BF16)

        m = _dot_nt(jnp.concatenate([at_bd, rt_bd], axis=0),
                    jnp.concatenate([bt_bd, kt_bd], axis=0))
        yield
        a_ab = jnp.where(lower, m[:n2, :n2], 0.0)
        a_ak = jnp.where(lower, m[:n2, n2:], 0.0).astype(BF16)
        a_r = jnp.where(jnp.concatenate([lower_eq, lower_eq], axis=1), m[n2:, :], 0.0).astype(BF16)

        x = eye + a_ab
        pw = a_ab.astype(BF16)
        pw = _dot(pw, pw)
        yield
        pw = pw.astype(BF16)
        steps = int(math.log2(L)) - 1
        for s in range(steps):
            if s + 1 < steps:
                px = _dot(pw, jnp.concatenate([pw, x.astype(BF16)], axis=1))
                yield
                pw = px[:, :n2].astype(BF16)
                x = x + px[:, n2:]
            else:
                px = _dot(pw, x)
                yield
                x = x + px

        st = st_ref[bi * tiles + p]
        rhs = _dot_nt(at_bd, st) + _dot(a_ak, v_bf)
        y0 = _dot_nt(rt_bd, st)
        yield
        u = _dot(x, rhs)
        yield
        uv = jnp.concatenate([u.astype(BF16), v_bf], axis=0)
        y = y0 + _dot(a_r, uv)
        st_ref[bi * tiles + p] = st * jnp.exp(cw_last) + _dot_tn(uv, jnp.concatenate([bw_bd, kw_bd], axis=0))
        yield

        gw_bd = stack(jnp.broadcast_to(gw_p[:, ls], (L, LANES)))
        gb_bd = stack(jnp.broadcast_to(gb_p[:, ls], (L, LANES)))
        mean = jnp.sum(y, axis=-1, keepdims=True) * (1.0 / HEAD)
        d = jnp.where(own, y - mean, 0.0)
        var = jnp.sum(d * d, axis=-1, keepdims=True) * (1.0 / HEAD)
        yn = d * lax.rsqrt(var + RW_GN_EPS) * gw_bd + gb_bd
        bonus = jnp.sum(stack(r * k_mod * r_k), axis=-1, keepdims=True)
        yo = yn + bonus * v_bd
        y_ref[bi, sl, ls] = ((yo[:L] + yo[L:]) * g_ref[bi, sl, ls]).astype(BF16)

    def chunk(c, carry):
        sl = pl.ds(pl.multiple_of(c * L, L), L)
        running = []
        for bi in range(r_ref.shape[0]):
            lw = w_ref[bi, sl, :]
            cw = _dot_sel_l(tril_incl, lw)
            running += [one_tile(bi, p, sl, cw[:, p * LANES:(p + 1) * LANES],
                                 lw[:, p * LANES:(p + 1) * LANES]) for p in range(tiles)]
        while running:
            running = [t for t in running if next(t, "done") != "done"]
        return carry

    lax.fori_loop(0, tb // L, chunk, 0)


def _rwkv_scan(r, w, k, v, a, g, k_k, k_a, r_k, gn_w, gn_b):
    b, seq, width = r.shape
    gl = RW_TILES * LANES
    tb = _row_tile(seq, RW_TIME_BLOCK)
    nb = RW_BATCH if b % RW_BATCH == 0 else 1
    blk = pl.BlockSpec((nb, tb, gl), lambda i, j, t: (i, t, j))
    par = pl.BlockSpec((1, gl), lambda i, j, t: (0, j))
    row = lambda p: p.reshape(1, -1)
    return pl.pallas_call(
        _rwkv_scan_kernel,
        grid=(b // nb, width // gl, seq // tb),
        in_specs=[blk] * 6 + [par] * 5,
        out_specs=blk,
        out_shape=jax.ShapeDtypeStruct((b, seq, width), BF16),
        scratch_shapes=[pltpu.VMEM((nb * RW_TILES, LANES, LANES), F32)],
        compiler_params=_cparams("parallel", "parallel", "arbitrary"),
        name="rwkv_scan",
    )(r, w, k, v, a, g, row(k_k), row(k_a), row(r_k), row(gn_w), row(gn_b))


def _out_proj_kernel(x_ref, yg_ref, w_ref, o_ref):
    o_ref[...] = x_ref[...] + jnp.dot(yg_ref[...], w_ref[...], preferred_element_type=F32)


def _out_proj(x2d, yg2d, w_out, seq):
    n, d = x2d.shape
    width = yg2d.shape[1]
    tm = _row_tile(seq, 512)
    return pl.pallas_call(
        _out_proj_kernel,
        grid=(n // tm,),
        in_specs=[pl.BlockSpec((tm, d), lambda i: (i, 0)),
                  pl.BlockSpec((tm, width), lambda i: (i, 0)),
                  _const_spec((width, d))],
        out_specs=pl.BlockSpec((tm, d), lambda i: (i, 0)),
        out_shape=jax.ShapeDtypeStruct((n, d), F32),
        compiler_params=_cparams("parallel"),
        name="out_proj",
    )(x2d, yg2d, w_out.astype(BF16))


def _rwkv_layer(x, norm_g, p, v_first):
    mu, w_in, w_up, w0, a_up, a0, k_k, k_a, r_k, gn_w, gn_b, w_out = p[:12]
    b, seq, d = x.shape
    x2d = x.reshape(b * seq, d)
    if v_first is None:
        r, w, k, v, a, g = _rwkv_in(x2d, seq, norm_g, mu, w_in, w_up, w0, a_up, a0)
        v_first = v
    else:
        v_up, v0 = p[12:]
        r, w, k, v, a, g = _rwkv_in(x2d, seq, norm_g, mu, w_in, w_up, w0, a_up, a0,
                                    v_up, v0, v_first)
    width = r.shape[1]
    sh = lambda t: t.reshape(b, seq, width)
    yg = _rwkv_scan(sh(r), sh(w), sh(k), sh(v), sh(a), sh(g), k_k, k_a, r_k.reshape(-1),
                    gn_w, gn_b)
    out = _out_proj(x2d, yg.reshape(b * seq, width), w_out, seq)
    return out.reshape(b, seq, d), v_first


def _norm_proj_kernel(acts, x_ref, g_ref, *refs):
    n = len(acts)
    w_refs, o_refs = refs[:n], refs[n:]
    h = _rms(x_ref[...], g_ref[...]).astype(BF16)
    for i in sorted(range(n), key=lambda i: not acts[i]):
        y = jnp.dot(h, w_refs[i][...], preferred_element_type=F32)
        o_refs[i][...] = _silu(y) if acts[i] else y


def _norm_proj(x2d, seq, norm_g, weights, acts):
    n, d = x2d.shape
    tm = _row_tile(seq, 256)
    weights = [w.astype(BF16) for w in weights]
    return pl.pallas_call(
        functools.partial(_norm_proj_kernel, tuple(acts)),
        grid=(n // tm,),
        in_specs=[pl.BlockSpec((tm, d), lambda i: (i, 0)), _const_spec((1, d))]
                 + [_const_spec(w.shape) for w in weights],
        out_specs=[pl.BlockSpec((tm, w.shape[1]), lambda i: (i, 0)) for w in weights],
        out_shape=[jax.ShapeDtypeStruct((n, w.shape[1]), F32) for w in weights],
        compiler_params=_cparams("parallel"),
        name="norm_proj",
    )(x2d, norm_g.reshape(1, d), *weights)


def _sb_in_kernel(x_ref, g_ref, wq_ref, wk_ref, wv_ref, wg_ref, qn_ref, kn_ref,
                  q_o, k_o, v_o, g_o):
    h = _rms(x_ref[...], g_ref[...]).astype(BF16)
    tm, width = q_o.shape
    head0 = lax.broadcasted_iota(jnp.int32, (tm, LANES), 1) < HEAD

    def head_rms(y, w_row, scale, o_ref):
        for j in range(width // LANES):
            ls = slice(j * LANES, (j + 1) * LANES)
            t = y[:, ls]
            sq = t * t
            s0 = jnp.sum(jnp.where(head0, sq, 0.0), axis=-1, keepdims=True)
            s1 = jnp.sum(jnp.where(head0, 0.0, sq), axis=-1, keepdims=True)
            inv = jnp.where(head0, lax.rsqrt(s0 * (1.0 / HEAD) + NORM_EPS),
                            lax.rsqrt(s1 * (1.0 / HEAD) + NORM_EPS))
            o_ref[:, ls] = (t * inv * (w_row[:, ls] * scale)).astype(BF16)

    g_o[...] = _silu(jnp.dot(h, wg_ref[...], preferred_element_type=F32))
    head_rms(jnp.dot(h, wq_ref[...], preferred_element_type=F32), qn_ref[...],
             1.0 / math.sqrt(HEAD), q_o)
    head_rms(jnp.dot(h, wk_ref[...], preferred_element_type=F32), kn_ref[...], 1.0, k_o)
    v_o[...] = jnp.dot(h, wv_ref[...], preferred_element_type=F32).astype(BF16)


def _sb_in(x2d, seq, norm_g, w_in, q_norm, k_norm, width):
    n, d = x2d.shape
    tm = _row_tile(seq, 256)
    ws = [w_in[:, i * width:(i + 1) * width].astype(BF16) for i in range(4)]
    tile_w = lambda t: jnp.tile(t, width // HEAD).reshape(1, width)
    out_spec = pl.BlockSpec((tm, width), lambda i: (i, 0))
    return pl.pallas_call(
        _sb_in_kernel,
        grid=(n // tm,),
        in_specs=[pl.BlockSpec((tm, d), lambda i: (i, 0)), _const_spec((1, d))]
                 + [_const_spec((d, width))] * 4 + [_const_spec((1, width))] * 2,
        out_specs=[out_spec] * 4,
        out_shape=[jax.ShapeDtypeStruct((n, width), BF16)] * 3
                  + [jax.ShapeDtypeStruct((n, width), F32)],
        compiler_params=_cparams("parallel"),
        name="sb_in",
    )(x2d, norm_g.reshape(1, d), *ws, tile_w(q_norm), tile_w(k_norm))


def _ssd_kernel(x_ref, z_ref, u_ref, dt_ref, cw_ref, cb_ref, dtb_ref, alog_ref, dsk_ref,
                gnw_ref, exp_ref, wout_ref, o_ref, ext_ref, h_ref, y_ref):
    L = SSD_CHUNK
    width = z_ref.shape[-1]
    heads_per_group = width // HEAD // SSD_GROUPS
    gw = heads_per_group * HEAD
    c = pl.program_id(1)

    @pl.when(c == 0)
    def _():
        ext_ref[...] = jnp.zeros_like(ext_ref)
        h_ref[...] = jnp.zeros_like(h_ref)

    u = u_ref[0]
    cdim = u.shape[1]
    u3 = u.reshape(L // 8, 8, cdim)
    tail = ext_ref[...]
    sub = lax.broadcasted_iota(jnp.int32, (L // 8, 8, cdim), 1)
    conv = cb_ref[...] + cw_ref[SSD_CONV - 1:SSD_CONV, :] * u
    for j in range(SSD_CONV - 1):
        s = SSD_CONV - 1 - j
        rot = pltpu.roll(u3, s, 1)
        before = jnp.concatenate([pltpu.roll(tail, s, 0)[None], rot[:-1]], axis=0)
        shifted = jnp.where(sub < s, before, rot).reshape(L, cdim)
        conv = conv + cw_ref[j:j + 1, :] * shifted
    ext_ref[...] = u[L - 8:, :]
    xbc = _silu(conv)
    xs = xbc[:, :width]
    n_bc = SSD_GROUPS * SSD_STATE
    bm = xbc[:, width:width + n_bc]
    cm = xbc[:, width + n_bc:]

    x_in = dt_ref[0] + dtb_ref[...]
    dt = jnp.maximum(x_in, 0.0) + jnp.log1p(jnp.exp(-jnp.abs(x_in)))
    a = dt * (-jnp.exp(alog_ref[...]))
    ri = lax.broadcasted_iota(jnp.int32, (L, L), 0)
    ci = lax.broadcasted_iota(jnp.int32, (L, L), 1)
    causal = ri >= ci
    ac = _dot_sel_l(causal.astype(BF16), a)
    ac_t = ac.T
    expand = exp_ref[...]
    dt_w = _dot_sel_r(dt, expand)
    ac_w = _dot_sel_r(ac, expand)
    ac_last_w = ac_w[L - 1:L, :]
    x_dt = xs * dt_w
    e_out = jnp.exp(ac_w)
    xd = x_dt * jnp.exp(ac_last_w - ac_w)
    e_last = jnp.exp(ac_last_w)
    lane_g = lax.broadcasted_iota(jnp.int32, (L, gw), 1) // HEAD

    def one_group(g):
        bg = bm[:, g * SSD_STATE:(g + 1) * SSD_STATE].astype(BF16)
        cg = cm[:, g * SSD_STATE:(g + 1) * SSD_STATE].astype(BF16)
        gs = slice(g * gw, (g + 1) * gw)
        cb = _dot_nt(cg, bg)
        h_prev = h_ref[g]
        y_off = _dot(cg, h_prev)
        h_new = _dot_tn(bg, xd[:, gs])
        yield
        x_g = x_dt[:, gs].astype(BF16)
        lms, xms = [], []
        for r in range(heads_per_group):
            hd = g * heads_per_group + r
            seg = ac[:, hd:hd + 1] - ac_t[hd:hd + 1, :]
            lms.append((cb * jnp.exp(jnp.where(causal, seg, -1e30))).astype(BF16))
            xms.append(jnp.where(lane_g == r, x_g, jnp.zeros_like(x_g)))
        y_diag = jnp.dot(jnp.concatenate(lms, axis=1), jnp.concatenate(xms, axis=0),
                         preferred_element_type=F32)
        h_ref[g] = h_prev * e_last[:, gs] + h_new
        yield
        y_ref[:, gs] = y_off * e_out[:, gs] + y_diag

    running = [one_group(g) for g in range(SSD_GROUPS)]
    while running:
        running = [t for t in running if next(t, "done") != "done"]

    y = (y_ref[...] + xs * dsk_ref[...]) * _silu(z_ref[0])
    for g in range(SSD_GROUPS):
        gs = slice(g * gw, (g + 1) * gw)
        yg = y[:, gs]
        ms = jnp.mean(yg * yg, axis=-1, keepdims=True)
        y_ref[:, gs] = yg * lax.rsqrt(ms + NORM_EPS)
    yn = (y_ref[...] * gnw_ref[...]).astype(BF16)
    o_ref[0] = x_ref[0] + jnp.dot(yn, wout_ref[...], preferred_element_type=F32)


def _mamba_layer(x, norm_g, p):
    w_in, conv_w, conv_b, dt_bias, a_log, d_skip, gnorm_w, w_out = p
    b, seq, d = x.shape
    width = w_out.shape[0]
    heads = d_skip.shape[0]
    conv_dim = conv_w.shape[1]
    L = SSD_CHUNK
    x2d = x.reshape(b * seq, d)
    w_dt = jnp.pad(w_in[:, width + conv_dim:], ((0, 0), (0, LANES - heads)))
    z, u, dt = _norm_proj(x2d, seq, norm_g,
                          [w_in[:, :width], w_in[:, width:width + conv_dim], w_dt],
                          [False, False, False])
    pad_h = lambda v: jnp.pad(v, (0, LANES - heads)).reshape(1, LANES)
    expand = (jnp.arange(LANES)[:, None] == (jnp.arange(width) // HEAD)[None, :]).astype(BF16)
    cw8 = jnp.pad(conv_w, ((0, 8 - conv_w.shape[0]), (0, 0)))
    gw = width // SSD_GROUPS
    blk = lambda n: pl.BlockSpec((1, L, n), lambda i, j: (i, j, 0))
    out = pl.pallas_call(
        _ssd_kernel,
        grid=(b, seq // L),
        in_specs=[blk(d), blk(width), blk(conv_dim), blk(LANES),
                  _const_spec((8, conv_dim)), _const_spec((1, conv_dim)),
                  _const_spec((1, LANES)), _const_spec((1, LANES)),
                  _const_spec((1, width)), _const_spec((1, width)),
                  _const_spec((LANES, width)), _const_spec((width, d))],
        out_specs=blk(d),
        out_shape=jax.ShapeDtypeStruct((b, seq, d), F32),
        scratch_shapes=[pltpu.VMEM((8, conv_dim), F32),
                        pltpu.VMEM((SSD_GROUPS, SSD_STATE, gw), F32),
                        pltpu.VMEM((L, width), F32)],
        compiler_params=_cparams("parallel", "arbitrary"),
        name="ssd",
    )(x, z.reshape(b, seq, width), u.reshape(b, seq, conv_dim), dt.reshape(b, seq, LANES),
      cw8, conv_b.reshape(1, -1), pad_h(dt_bias), pad_h(a_log),
      jnp.repeat(d_skip, HEAD).reshape(1, -1), gnorm_w.reshape(1, -1), expand,
      w_out.astype(BF16))
    return out


def _sb_kernel(q_ref, ks_ref, vs_ref, g_ref, o_ref):
    blk = q_ref.shape[1]
    i = pl.program_id(2)
    lane = lax.broadcasted_iota(jnp.int32, (blk, LANES), 1)
    head0 = lane < HEAD
    ks_ref, vs_ref = ks_ref.at[0], vs_ref.at[0]
    q = q_ref[0]
    zq = jnp.zeros_like(q)
    q_bd = jnp.concatenate([jnp.where(head0, q, zq), jnp.where(head0, zq, q)], axis=0)
    gr = min(SB_ROWS, blk)
    ngroups = 2 * blk // gr
    ji = lax.broadcasted_iota(jnp.int32, (blk, blk), 0)
    si = lax.broadcasted_iota(jnp.int32, (blk, blk), 1)
    suffix = (ji > si).astype(BF16)

    def group(g, j, get_run, diagonal, out, present=None):
        first = (g * gr) % blk
        nk = first + gr if diagonal else blk
        sl = pl.ds(pl.multiple_of(j * blk, blk), nk)
        z = _dot_nt(q_bd[g * gr:(g + 1) * gr], ks_ref[sl, :])
        yield
        log_beta = jnp.minimum(z, 0.0) - jnp.log(1.0 + jnp.exp(-jnp.abs(z)))
        log_1m = log_beta - z
        if diagonal:
            strict = (lax.broadcasted_iota(jnp.int32, (gr, nk), 0) + first
                      > lax.broadcasted_iota(jnp.int32, (gr, nk), 1))
            log_1m = jnp.where(strict, log_1m, 0.0)
        tail = _dot_sel_r(log_1m, suffix[:nk, :nk], terms=2)
        total = jnp.sum(log_1m, axis=-1, keepdims=True)
        yield
        run = get_run()
        run_in = run if present is None else jnp.where(present, run, -1e30)
        att = jnp.exp(log_beta + tail + run_in)
        if diagonal:
            att = jnp.where(strict, att, 0.0)
        if present is not None:
            total = jnp.where(present, total, 0.0)
        out[g] = (run + total, _dot(att, vs_ref[sl, :]))
        yield

    def lockstep(gens):
        while gens:
            gens = [t for t in gens if next(t, "done") != "done"]

    def live(runs):
        top = functools.reduce(jnp.maximum, [jnp.max(r) for r in runs])
        return (top > -SB_DEAD).astype(jnp.int32)

    zero = jnp.zeros((gr, 1), F32)
    d_out, p_out = {}, {}
    lockstep([group(g, i, lambda: zero, True, d_out) for g in range(ngroups)]
             + [group(g, jnp.maximum(i - 1, 0), lambda g=g: d_out[g][0], False, p_out, i > 0)
                for g in range(ngroups)])
    runs = [p_out[g][0] for g in range(ngroups)]
    accs = [d_out[g][1] + p_out[g][1] for g in range(ngroups)]

    def cond(c):
        jj, _, _, go = c
        return jnp.logical_and(jj <= i, go > 0)

    def body(c):
        jj, runs, accs, _ = c
        out = {}
        lockstep([group(g, i - jj, lambda g=g: runs[g], False, out) for g in range(ngroups)])
        runs = [out[g][0] for g in range(ngroups)]
        return jj + 1, runs, [accs[g] + out[g][1] for g in range(ngroups)], live(runs)

    _, _, accs, _ = lax.while_loop(cond, body, (jnp.int32(2), runs, accs, live(runs)))
    acc = jnp.concatenate(accs, axis=0)
    o_ref[0] = (jnp.where(head0, acc[:blk], acc[blk:]) * g_ref[0]).astype(BF16)


def _sb_layer(x, norm_g, p):
    w_in, q_norm, k_norm, w_out = p
    b, seq, d = x.shape
    width = w_out.shape[0]
    x2d = x.reshape(b * seq, d)
    q, k, v, g = _sb_in(x2d, seq, norm_g, w_in, q_norm, k_norm, width)
    sh = lambda t: t.reshape(b, seq, width)
    full = pl.BlockSpec((1, seq, LANES), lambda bi, pi, qi: (bi, 0, pi))
    blk = _row_tile(seq, SB_BLOCK)
    qblk = pl.BlockSpec((1, blk, LANES), lambda bi, pi, qi: (bi, qi, pi))
    o = pl.pallas_call(
        _sb_kernel,
        grid=(b, width // LANES, seq // blk),
        in_specs=[qblk, full, full, qblk],
        out_specs=qblk,
        out_shape=jax.ShapeDtypeStruct((b, seq, width), BF16),
        compiler_params=_cparams("parallel", "parallel", "parallel"),
        name="sb_attn",
    )(sh(q), sh(k), sh(v), sh(g))
    out = _out_proj(x2d, o.reshape(b * seq, width), w_out, seq)
    return out.reshape(b, seq, d)


def kernel(x, l0_norm, l0_mu, l0_w_in, l0_w_up, l0_w0, l0_a_up, l0_a0, l0_k_k, l0_k_a, l0_r_k, l0_gn_w, l0_gn_b, l0_w_out, l1_norm, l1_w_in, l1_conv_w, l1_conv_b, l1_dt_bias, l1_a_log, l1_d_skip, l1_gnorm_w, l1_w_out, l2_norm, l2_w_in, l2_q_norm, l2_k_norm, l2_w_out, l3_norm, l3_mu, l3_w_in, l3_w_up, l3_w0, l3_a_up, l3_a0, l3_k_k, l3_k_a, l3_r_k, l3_gn_w, l3_gn_b, l3_w_out, l3_v_up, l3_v0):
    p0 = (l0_mu, l0_w_in, l0_w_up, l0_w0, l0_a_up, l0_a0, l0_k_k, l0_k_a, l0_r_k,
          l0_gn_w, l0_gn_b, l0_w_out)
    p3 = (l3_mu, l3_w_in, l3_w_up, l3_w0, l3_a_up, l3_a0, l3_k_k, l3_k_a, l3_r_k,
          l3_gn_w, l3_gn_b, l3_w_out, l3_v_up, l3_v0)
    p1 = (l1_w_in, l1_conv_w, l1_conv_b, l1_dt_bias, l1_a_log, l1_d_skip, l1_gnorm_w, l1_w_out)
    p2 = (l2_w_in, l2_q_norm, l2_k_norm, l2_w_out)
    x, v_first = _rwkv_layer(x, l0_norm, p0, None)
    x = _mamba_layer(x, l1_norm, p1)
    x = _sb_layer(x, l2_norm, p2)
    x, _ = _rwkv_layer(x, l3_norm, p3, v_first)
    return x
```

```python
import functools
import math

import jax
import jax.numpy as jnp
from jax import lax
from jax.experimental import pallas as pl
from jax.experimental.pallas import tpu as pltpu

F32 = jnp.float32
BF16 = jnp.bfloat16

LANES = 128
HEAD = 64
NORM_EPS = 1e-6
RW_GN_EPS = 64e-5
PROJ_ROWS = 512
OUT_ROWS = 1024
RW_CHUNK = 64
RW_TILES = 8
RW_BATCH = 4
RW_TIME_BLOCK = 128
SSD_CHUNK = 128
SSD_STATE = 128
SSD_GROUPS = 8
SSD_CONV = 4
SB_BLOCK = 256
SB_ROWS = 128
SB_DEAD = 104.0
VMEM_LIMIT = 56 * 1024 * 1024


def _cparams(*sem):
    return pltpu.CompilerParams(dimension_semantics=sem, vmem_limit_bytes=VMEM_LIMIT)


def _dot(a, b):
    return jnp.dot(a.astype(BF16), b.astype(BF16), preferred_element_type=F32)


def _dot_nt(a, b):
    return lax.dot_general(a.astype(BF16), b.astype(BF16), (((1,), (1,)), ((), ())),
                           preferred_element_type=F32)


def _dot_tn(a, b):
    return lax.dot_general(a.astype(BF16), b.astype(BF16), (((0,), (0,)), ((), ())),
                           preferred_element_type=F32)


def _split(x, terms):
    out = []
    rem = x
    for i in range(terms):
        p = rem.astype(BF16)
        out.append(p)
        if i + 1 < terms:
            rem = rem - p.astype(F32)
    return out


def _dot_sel_l(sel_bf16, x, terms=3):
    acc = None
    for p in _split(x, terms):
        d = jnp.dot(sel_bf16, p, preferred_element_type=F32)
        acc = d if acc is None else acc + d
    return acc


def _dot_sel_r(x, sel_bf16, terms=3):
    acc = None
    for p in _split(x, terms):
        d = jnp.dot(p, sel_bf16, preferred_element_type=F32)
        acc = d if acc is None else acc + d
    return acc


def _dot3(a, b):
    a_hi, a_lo = _split(a, 2)
    b_hi, b_lo = _split(b, 2)
    return (jnp.dot(a_hi, b_hi, preferred_element_type=F32)
            + jnp.dot(a_hi, b_lo, preferred_element_type=F32)
            + jnp.dot(a_lo, b_hi, preferred_element_type=F32))


def _rms(x, g):
    return x * lax.rsqrt(jnp.mean(x * x, axis=-1, keepdims=True) + NORM_EPS) * g


def _sigmoid(x):
    return 1.0 / (1.0 + jnp.exp(-x))


def _silu(x):
    return x * _sigmoid(x)


def _row_tile(t, target):
    tm = min(t, target)
    while t % tm:
        tm //= 2
    return tm


def _const_spec(shape):
    nd = len(shape)
    return pl.BlockSpec(shape, lambda *_: (0,) * nd, pipeline_mode=pl.Buffered(1))


def _rwkv_in_kernel(tiles_per_seq, has_vres, x_ref, xp_ref, g_ref, mu_ref, w4_ref, ws_ref,
                    wup_ref, w0_ref, aup_ref, a0_ref, *rest):
    if has_vres:
        wv_ref, vup_ref, v0_ref, vf_ref, r_o, w_o, k_o, v_o, a_o, g_o = rest
    else:
        r_o, w_o, k_o, v_o, a_o, g_o = rest
    width = r_o.shape[-1]
    g = g_ref[...]
    h = _rms(x_ref[...], g)
    hp = _rms(xp_ref[...], g)[7:8, :]
    first = (pl.program_id(0) % tiles_per_seq) == 0
    hp = jnp.where(first, 0.0, hp)
    rows = lax.broadcasted_iota(jnp.int32, h.shape, 0)
    h_prev = jnp.where(rows == 0, hp, pltpu.roll(h, 1, 0))
    dx = h_prev - h
    mu = mu_ref[...]

    def mix(i):
        return (h + dx * mu[i:i + 1, :]).astype(BF16)

    w_lo = jnp.tanh(jnp.dot(mix(1), ws_ref[...], preferred_element_type=F32))
    a_lo = jnp.dot(mix(4), ws_ref[...], preferred_element_type=F32)
    w_raw = w0_ref[...] + _dot(w_lo, wup_ref[...])
    w_o[...] = -_sigmoid(w_raw) * math.exp(-0.5)
    a_o[...] = _sigmoid(a0_ref[...] + _dot(a_lo, aup_ref[...]))
    g_o[...] = _silu(jnp.dot(mix(5), w4_ref[:, 3 * width:4 * width], preferred_element_type=F32))
    x_v = mix(3)
    v = jnp.dot(x_v, w4_ref[:, 2 * width:3 * width], preferred_element_type=F32)
    if has_vres:
        v_lo = jnp.dot(x_v, wv_ref[...], preferred_element_type=F32)
        gate = _sigmoid(v0_ref[...] + _dot(v_lo, vup_ref[...]))
        v = v + (vf_ref[...] - v) * gate
    v_o[...] = v
    r_o[...] = jnp.dot(mix(0), w4_ref[:, 0:width], preferred_element_type=F32)
    k_o[...] = jnp.dot(mix(2), w4_ref[:, width:2 * width], preferred_element_type=F32)


def _rwkv_in(x2d, seq, norm_g, mu, w_in, w_up, w0, a_up, a0, v_up=None, v0=None, v_first=None):
    n, d = x2d.shape
    width = w_up.shape[1]
    rank_w, rank_a = w_up.shape[0], a_up.shape[0]
    has_vres = v_up is not None
    tm = _row_tile(seq, PROJ_ROWS)
    tiles_per_seq = seq // tm
    w4 = w_in[:, :4 * width].astype(BF16)
    ws = w_in[:, 4 * width:4 * width + rank_w + rank_a].astype(BF16)
    wup = jnp.concatenate([w_up, jnp.zeros((rank_a, width), F32)], 0).astype(BF16)
    aup = jnp.concatenate([jnp.zeros((rank_w, width), F32), a_up], 0).astype(BF16)
    mu8 = jnp.concatenate([mu, jnp.zeros((2, d), F32)], 0)
    row = lambda p: p.reshape(1, -1)
    tile = pl.BlockSpec((tm, d), lambda i: (i, 0))
    wtile = pl.BlockSpec((tm, width), lambda i: (i, 0))
    prev = pl.BlockSpec((8, d), lambda i: (jnp.maximum(i * (tm // 8) - 1, 0), 0))
    args = [x2d, x2d, row(norm_g), mu8, w4, ws, wup, row(w0), aup, row(a0)]
    specs = [tile, prev, _const_spec((1, d)), _const_spec((8, d)), _const_spec(w4.shape),
             _const_spec(ws.shape), _const_spec(wup.shape), _const_spec((1, width)),
             _const_spec(aup.shape), _const_spec((1, width))]
    if has_vres:
        rank_v = v_up.shape[0]
        wv = jnp.pad(w_in[:, 4 * width + rank_w + rank_a:], ((0, 0), (0, LANES - rank_v))).astype(BF16)
        vup = jnp.pad(v_up, ((0, LANES - rank_v), (0, 0))).astype(BF16)
        args += [wv, vup, row(v0), v_first]
        specs += [_const_spec(wv.shape), _const_spec(vup.shape), _const_spec((1, width)), wtile]
    out_shape = [jax.ShapeDtypeStruct((n, width), F32)] * 6
    return pl.pallas_call(
        functools.partial(_rwkv_in_kernel, tiles_per_seq, has_vres),
        grid=(n // tm,),
        in_specs=specs,
        out_specs=[wtile] * 6,
        out_shape=out_shape,
        compiler_params=_cparams("parallel"),
        name="rwkv_in",
    )(*args)


def _rwkv_scan_kernel(r_ref, w_ref, k_ref, v_ref, a_ref, g_ref, kk_p, ka_p, rk_p, gw_p, gb_p,
                      y_ref, st_ref):
    L = RW_CHUNK
    tb = r_ref.shape[1]
    tiles = r_ref.shape[2] // LANES
    n2 = 2 * L
    rows = lax.broadcasted_iota(jnp.int32, (n2, LANES), 0)
    cols = lax.broadcasted_iota(jnp.int32, (n2, LANES), 1)
    own = (rows < L) == (cols < HEAD)
    lower = rows > cols
    lower_eq = rows >= cols
    eye = (rows == cols).astype(F32)
    ri = lax.broadcasted_iota(jnp.int32, (L, L), 0)
    ci = lax.broadcasted_iota(jnp.int32, (L, L), 1)
    tril_incl = (ri >= ci).astype(BF16)

    def dup(x):
        return jnp.concatenate([x, x], axis=0)

    def stack(x):
        return jnp.where(own, dup(x), 0.0)

    @pl.when(pl.program_id(2) == 0)
    def _():
        st_ref[...] = jnp.zeros_like(st_ref)

    def one_tile(bi, p, sl, cw, lw):
        ls = slice(p * LANES, (p + 1) * LANES)
        k_k, k_a, r_k = kk_p[:, ls], ka_p[:, ls], rk_p[:, ls]
        r = r_ref[bi, sl, ls]
        k = k_ref[bi, sl, ls]
        v = v_ref[bi, sl, ls]
        a = a_ref[bi, sl, ls]
        cw_last = cw[L - 1:L, :]
        e_in = jnp.exp(cw)
        e_ex = jnp.exp(cw - lw)
        e_inv = jnp.exp(-cw)
        e_end = jnp.exp(cw_last - cw)
        kk_bd = stack(k * k_k)
        nrm = jnp.sqrt(jnp.sum(kk_bd * kk_bd, axis=-1, keepdims=True))
        kk_bd = kk_bd * (1.0 / jnp.maximum(nrm, 1e-12))
        k_mod = k * (1.0 + (a - 1.0) * k_a)
        at_bd = (-kk_bd * dup(e_ex)).astype(BF16)
        bt_bd = (kk_bd * dup(a * e_inv)).astype(BF16)
        kt_bd = stack(k_mod * e_inv).astype(BF16)
        rt_bd = stack(r * e_in).astype(BF16)
        v_bd = stack(v)
        v_bf = v_bd.astype(BF16)
        bw_bd = (kk_bd * dup(a * e_end)).astype(BF16)
        kw_bd = stack(k_mod * e_end).astype(BF16)

        m = _dot_nt(jnp.concatenate([at_bd, rt_bd], axis=0),
                    jnp.concatenate([bt_bd, kt_bd], axis=0))
        yield
        a_ab = jnp.where(lower, m[:n2, :n2], 0.0)
        a_ak = jnp.where(lower, m[:n2, n2:], 0.0).astype(BF16)
        a_r = jnp.where(jnp.concatenate([lower_eq, lower_eq], axis=1), m[n2:, :], 0.0).astype(BF16)

        x = eye + a_ab
        pw = a_ab.astype(BF16)
        pw = _dot(pw, pw)
        yield
        pw = pw.astype(BF16)
        steps = int(math.log2(L)) - 1
        for s in range(steps):
            if s + 1 < steps:
                px = _dot(pw, jnp.concatenate([pw, x.astype(BF16)], axis=1))
                yield
                pw = px[:, :n2].astype(BF16)
                x = x + px[:, n2:]
            else:
                px = _dot(pw, x)
                yield
                x = x + px

        st = st_ref[bi * tiles + p]
        rhs = _dot_nt(at_bd, st) + _dot(a_ak, v_bf)
        y0 = _dot_nt(rt_bd, st)
        yield
        u = _dot(x, rhs)
        yield
        uv = jnp.concatenate([u.astype(BF16), v_bf], axis=0)
        y = y0 + _dot(a_r, uv)
        st_ref[bi * tiles + p] = st * jnp.exp(cw_last) + _dot_tn(uv, jnp.concatenate([bw_bd, kw_bd], axis=0))
        yield

        gw_bd = stack(jnp.broadcast_to(gw_p[:, ls], (L, LANES)))
        gb_bd = stack(jnp.broadcast_to(gb_p[:, ls], (L, LANES)))
        mean = jnp.sum(y, axis=-1, keepdims=True) * (1.0 / HEAD)
        d = jnp.where(own, y - mean, 0.0)
        var = jnp.sum(d * d, axis=-1, keepdims=True) * (1.0 / HEAD)
        yn = d * lax.rsqrt(var + RW_GN_EPS) * gw_bd + gb_bd
        bonus = jnp.sum(stack(r * k_mod * r_k), axis=-1, keepdims=True)
        yo = yn + bonus * v_bd
        y_ref[bi, sl, ls] = ((yo[:L] + yo[L:]) * g_ref[bi, sl, ls]).astype(BF16)

    def chunk(c, carry):
        sl = pl.ds(pl.multiple_of(c * L, L), L)
        running = []
        for bi in range(r_ref.shape[0]):
            lw = w_ref[bi, sl, :]
            cw = _dot_sel_l(tril_incl, lw)
            running += [one_tile(bi, p, sl, cw[:, p * LANES:(p + 1) * LANES],
                                 lw[:, p * LANES:(p + 1) * LANES]) for p in range(tiles)]
        while running:
            running = [t for t in running if next(t, "done") != "done"]
        return carry

    lax.fori_loop(0, tb // L, chunk, 0)


def _rwkv_scan(r, w, k, v, a, g, k_k, k_a, r_k, gn_w, gn_b):
    b, seq, width = r.shape
    gl = RW_TILES * LANES
    tb = _row_tile(seq, RW_TIME_BLOCK)
    nb = RW_BATCH if b % RW_BATCH == 0 else 1
    blk = pl.BlockSpec((nb, tb, gl), lambda i, j, t: (i, t, j))
    par = pl.BlockSpec((1, gl), lambda i, j, t: (0, j))
    row = lambda p: p.reshape(1, -1)
    return pl.pallas_call(
        _rwkv_scan_kernel,
        grid=(b // nb, width // gl, seq // tb),
        in_specs=[blk] * 6 + [par] * 5,
        out_specs=blk,
        out_shape=jax.ShapeDtypeStruct((b, seq, width), BF16),
        scratch_shapes=[pltpu.VMEM((nb * RW_TILES, LANES, LANES), F32)],
        compiler_params=_cparams("parallel", "parallel", "arbitrary"),
        name="rwkv_scan",
    )(r, w, k, v, a, g, row(k_k), row(k_a), row(r_k), row(gn_w), row(gn_b))


def _out_proj_kernel(x_ref, yg_ref, w_ref, o_ref):
    o_ref[...] = x_ref[...] + jnp.dot(yg_ref[...], w_ref[...], preferred_element_type=F32)


def _out_proj(x2d, yg2d, w_out, seq):
    n, d = x2d.shape
    width = yg2d.shape[1]
    tm = _row_tile(seq, OUT_ROWS)
    return pl.pallas_call(
        _out_proj_kernel,
        grid=(n // tm,),
        in_specs=[pl.BlockSpec((tm, d), lambda i: (i, 0)),
                  pl.BlockSpec((tm, width), lambda i: (i, 0)),
                  _const_spec((width, d))],
        out_specs=pl.BlockSpec((tm, d), lambda i: (i, 0)),
        out_shape=jax.ShapeDtypeStruct((n, d), F32),
        compiler_params=_cparams("parallel"),
        name="out_proj",
    )(x2d, yg2d, w_out.astype(BF16))


def _rwkv_layer(x, norm_g, p, v_first):
    mu, w_in, w_up, w0, a_up, a0, k_k, k_a, r_k, gn_w, gn_b, w_out = p[:12]
    b, seq, d = x.shape
    x2d = x.reshape(b * seq, d)
    if v_first is None:
        r, w, k, v, a, g = _rwkv_in(x2d, seq, norm_g, mu, w_in, w_up, w0, a_up, a0)
        v_first = v
    else:
        v_up, v0 = p[12:]
        r, w, k, v, a, g = _rwkv_in(x2d, seq, norm_g, mu, w_in, w_up, w0, a_up, a0,
                                    v_up, v0, v_first)
    width = r.shape[1]
    sh = lambda t: t.reshape(b, seq, width)
    yg = _rwkv_scan(sh(r), sh(w), sh(k), sh(v), sh(a), sh(g), k_k, k_a, r_k.reshape(-1),
                    gn_w, gn_b)
    out = _out_proj(x2d, yg.reshape(b * seq, width), w_out, seq)
    return out.reshape(b, seq, d), v_first


def _norm_proj_kernel(acts, x_ref, g_ref, *refs):
    n = len(acts)
    w_refs, o_refs = refs[:n], refs[n:]
    h = _rms(x_ref[...], g_ref[...]).astype(BF16)
    for i in sorted(range(n), key=lambda i: not acts[i]):
        y = jnp.dot(h, w_refs[i][...], preferred_element_type=F32)
        o_refs[i][...] = _silu(y) if acts[i] else y


def _norm_proj(x2d, seq, norm_g, weights, acts):
    n, d = x2d.shape
    tm = _row_tile(seq, PROJ_ROWS)
    weights = [w.astype(BF16) for w in weights]
    return pl.pallas_call(
        functools.partial(_norm_proj_kernel, tuple(acts)),
        grid=(n // tm,),
        in_specs=[pl.BlockSpec((tm, d), lambda i: (i, 0)), _const_spec((1, d))]
                 + [_const_spec(w.shape) for w in weights],
        out_specs=[pl.BlockSpec((tm, w.shape[1]), lambda i: (i, 0)) for w in weights],
        out_shape=[jax.ShapeDtypeStruct((n, w.shape[1]), F32) for w in weights],
        compiler_params=_cparams("parallel"),
        name="norm_proj",
    )(x2d, norm_g.reshape(1, d), *weights)


def _sb_in_kernel(x_ref, g_ref, wq_ref, wk_ref, wv_ref, wg_ref, qn_ref, kn_ref,
                  q_o, k_o, v_o, g_o):
    h = _rms(x_ref[...], g_ref[...]).astype(BF16)
    tm, width = q_o.shape
    head0 = lax.broadcasted_iota(jnp.int32, (tm, LANES), 1) < HEAD

    def head_rms(y, w_row, scale, o_ref):
        for j in range(width // LANES):
            ls = slice(j * LANES, (j + 1) * LANES)
            t = y[:, ls]
            sq = t * t
            s0 = jnp.sum(jnp.where(head0, sq, 0.0), axis=-1, keepdims=True)
            s1 = jnp.sum(jnp.where(head0, 0.0, sq), axis=-1, keepdims=True)
            inv = jnp.where(head0, lax.rsqrt(s0 * (1.0 / HEAD) + NORM_EPS),
                            lax.rsqrt(s1 * (1.0 / HEAD) + NORM_EPS))
            o_ref[:, ls] = (t * inv * (w_row[:, ls] * scale)).astype(BF16)

    g_o[...] = _silu(jnp.dot(h, wg_ref[...], preferred_element_type=F32))
    head_rms(jnp.dot(h, wq_ref[...], preferred_element_type=F32), qn_ref[...],
             1.0 / math.sqrt(HEAD), q_o)
    head_rms(jnp.dot(h, wk_ref[...], preferred_element_type=F32), kn_ref[...], 1.0, k_o)
    v_o[...] = jnp.dot(h, wv_ref[...], preferred_element_type=F32).astype(BF16)


def _sb_in(x2d, seq, norm_g, w_in, q_norm, k_norm, width):
    n, d = x2d.shape
    tm = _row_tile(seq, PROJ_ROWS)
    ws = [w_in[:, i * width:(i + 1) * width].astype(BF16) for i in range(4)]
    tile_w = lambda t: jnp.tile(t, width // HEAD).reshape(1, width)
    out_spec = pl.BlockSpec((tm, width), lambda i: (i, 0))
    return pl.pallas_call(
        _sb_in_kernel,
        grid=(n // tm,),
        in_specs=[pl.BlockSpec((tm, d), lambda i: (i, 0)), _const_spec((1, d))]
                 + [_const_spec((d, width))] * 4 + [_const_spec((1, width))] * 2,
        out_specs=[out_spec] * 4,
        out_shape=[jax.ShapeDtypeStruct((n, width), BF16)] * 3
                  + [jax.ShapeDtypeStruct((n, width), F32)],
        compiler_params=_cparams("parallel"),
        name="sb_in",
    )(x2d, norm_g.reshape(1, d), *ws, tile_w(q_norm), tile_w(k_norm))


def _ssd_kernel(x_ref, z_ref, u_ref, dt_ref, cw_ref, cb_ref, dtb_ref, alog_ref, dsk_ref,
                gnw_ref, exp_ref, wout_ref, o_ref, ext_ref, h_ref, y_ref):
    L = SSD_CHUNK
    width = z_ref.shape[-1]
    heads_per_group = width // HEAD // SSD_GROUPS
    gw = heads_per_group * HEAD
    c = pl.program_id(1)

    @pl.when(c == 0)
    def _():
        ext_ref[...] = jnp.zeros_like(ext_ref)
        h_ref[...] = jnp.zeros_like(h_ref)

    u = u_ref[0]
    cdim = u.shape[1]
    u3 = u.reshape(L // 8, 8, cdim)
    tail = ext_ref[...]
    sub = lax.broadcasted_iota(jnp.int32, (L // 8, 8, cdim), 1)
    conv = cb_ref[...] + cw_ref[SSD_CONV - 1:SSD_CONV, :] * u
    for j in range(SSD_CONV - 1):
        s = SSD_CONV - 1 - j
        rot = pltpu.roll(u3, s, 1)
        before = jnp.concatenate([pltpu.roll(tail, s, 0)[None], rot[:-1]], axis=0)
        shifted = jnp.where(sub < s, before, rot).reshape(L, cdim)
        conv = conv + cw_ref[j:j + 1, :] * shifted
    ext_ref[...] = u[L - 8:, :]
    xbc = _silu(conv)
    xs = xbc[:, :width]
    n_bc = SSD_GROUPS * SSD_STATE
    bm = xbc[:, width:width + n_bc]
    cm = xbc[:, width + n_bc:]

    x_in = dt_ref[0] + dtb_ref[...]
    dt = jnp.maximum(x_in, 0.0) + jnp.log1p(jnp.exp(-jnp.abs(x_in)))
    a = dt * (-jnp.exp(alog_ref[...]))
    ri = lax.broadcasted_iota(jnp.int32, (L, L), 0)
    ci = lax.broadcasted_iota(jnp.int32, (L, L), 1)
    causal = ri >= ci
    ac = _dot_sel_l(causal.astype(BF16), a)
    ac_t = ac.T
    expand = exp_ref[...]
    dt_w = _dot_sel_r(dt, expand)
    ac_w = _dot_sel_r(ac, expand)
    ac_last_w = ac_w[L - 1:L, :]
    x_dt = xs * dt_w
    e_out = jnp.exp(ac_w)
    xd = x_dt * jnp.exp(ac_last_w - ac_w)
    e_last = jnp.exp(ac_last_w)
    lane_g = lax.broadcasted_iota(jnp.int32, (L, gw), 1) // HEAD

    def one_group(g):
        bg = bm[:, g * SSD_STATE:(g + 1) * SSD_STATE].astype(BF16)
        cg = cm[:, g * SSD_STATE:(g + 1) * SSD_STATE].astype(BF16)
        gs = slice(g * gw, (g + 1) * gw)
        cb = _dot_nt(cg, bg)
        h_prev = h_ref[g]
        y_off = _dot(cg, h_prev)
        h_new = _dot_tn(bg, xd[:, gs])
        yield
        x_g = x_dt[:, gs].astype(BF16)
        lms, xms = [], []
        for r in range(heads_per_group):
            hd = g * heads_per_group + r
            seg = ac[:, hd:hd + 1] - ac_t[hd:hd + 1, :]
            lms.append((cb * jnp.exp(jnp.where(causal, seg, -1e30))).astype(BF16))
            xms.append(jnp.where(lane_g == r, x_g, jnp.zeros_like(x_g)))
        y_diag = jnp.dot(jnp.concatenate(lms, axis=1), jnp.concatenate(xms, axis=0),
                         preferred_element_type=F32)
        h_ref[g] = h_prev * e_last[:, gs] + h_new
        yield
        y_ref[:, gs] = y_off * e_out[:, gs] + y_diag

    running = [one_group(g) for g in range(SSD_GROUPS)]
    while running:
        running = [t for t in running if next(t, "done") != "done"]

    y = (y_ref[...] + xs * dsk_ref[...]) * _silu(z_ref[0])
    for g in range(SSD_GROUPS):
        gs = slice(g * gw, (g + 1) * gw)
        yg = y[:, gs]
        ms = jnp.mean(yg * yg, axis=-1, keepdims=True)
        y_ref[:, gs] = yg * lax.rsqrt(ms + NORM_EPS)
    yn = (y_ref[...] * gnw_ref[...]).astype(BF16)
    o_ref[0] = x_ref[0] + jnp.dot(yn, wout_ref[...], preferred_element_type=F32)


def _mamba_layer(x, norm_g, p):
    w_in, conv_w, conv_b, dt_bias, a_log, d_skip, gnorm_w, w_out = p
    b, seq, d = x.shape
    width = w_out.shape[0]
    heads = d_skip.shape[0]
    conv_dim = conv_w.shape[1]
    L = SSD_CHUNK
    x2d = x.reshape(b * seq, d)
    w_dt = jnp.pad(w_in[:, width + conv_dim:], ((0, 0), (0, LANES - heads)))
    z, u, dt = _norm_proj(x2d, seq, norm_g,
                          [w_in[:, :width], w_in[:, width:width + conv_dim], w_dt],
                          [False, False, False])
    pad_h = lambda v: jnp.pad(v, (0, LANES - heads)).reshape(1, LANES)
    expand = (jnp.arange(LANES)[:, None] == (jnp.arange(width) // HEAD)[None, :]).astype(BF16)
    cw8 = jnp.pad(conv_w, ((0, 8 - conv_w.shape[0]), (0, 0)))
    gw = width // SSD_GROUPS
    blk = lambda n: pl.BlockSpec((1, L, n), lambda i, j: (i, j, 0))
    out = pl.pallas_call(
        _ssd_kernel,
        grid=(b, seq // L),
        in_specs=[blk(d), blk(width), blk(conv_dim), blk(LANES),
                  _const_spec((8, conv_dim)), _const_spec((1, conv_dim)),
                  _const_spec((1, LANES)), _const_spec((1, LANES)),
                  _const_spec((1, width)), _const_spec((1, width)),
                  _const_spec((LANES, width)), _const_spec((width, d))],
        out_specs=blk(d),
        out_shape=jax.ShapeDtypeStruct((b, seq, d), F32),
        scratch_shapes=[pltpu.VMEM((8, conv_dim), F32),
                        pltpu.VMEM((SSD_GROUPS, SSD_STATE, gw), F32),
                        pltpu.VMEM((L, width), F32)],
        compiler_params=_cparams("parallel", "arbitrary"),
        name="ssd",
    )(x, z.reshape(b, seq, width), u.reshape(b, seq, conv_dim), dt.reshape(b, seq, LANES),
      cw8, conv_b.reshape(1, -1), pad_h(dt_bias), pad_h(a_log),
      jnp.repeat(d_skip, HEAD).reshape(1, -1), gnorm_w.reshape(1, -1), expand,
      w_out.astype(BF16))
    return out


def _sb_kernel(q_ref, ks_ref, vs_ref, g_ref, o_ref):
    blk = q_ref.shape[1]
    i = pl.program_id(2)
    lane = lax.broadcasted_iota(jnp.int32, (blk, LANES), 1)
    head0 = lane < HEAD
    ks_ref, vs_ref = ks_ref.at[0], vs_ref.at[0]
    q = q_ref[0]
    zq = jnp.zeros_like(q)
    q_bd = jnp.concatenate([jnp.where(head0, q, zq), jnp.where(head0, zq, q)], axis=0)
    gr = min(SB_ROWS, blk)
    ngroups = 2 * blk // gr
    ji = lax.broadcasted_iota(jnp.int32, (blk, blk), 0)
    si = lax.broadcasted_iota(jnp.int32, (blk, blk), 1)
    suffix = (ji > si).astype(BF16)

    def group(g, j, get_run, diagonal, out, present=None):
        first = (g * gr) % blk
        nk = first + gr if diagonal else blk
        sl = pl.ds(pl.multiple_of(j * blk, blk), nk)
        z = _dot_nt(q_bd[g * gr:(g + 1) * gr], ks_ref[sl, :])
        yield
        log_beta = jnp.minimum(z, 0.0) - jnp.log(1.0 + jnp.exp(-jnp.abs(z)))
        log_1m = log_beta - z
        if diagonal:
            strict = (lax.broadcasted_iota(jnp.int32, (gr, nk), 0) + first
                      > lax.broadcasted_iota(jnp.int32, (gr, nk), 1))
            log_1m = jnp.where(strict, log_1m, 0.0)
        tail = _dot_sel_r(log_1m, suffix[:nk, :nk], terms=2)
        total = jnp.sum(log_1m, axis=-1, keepdims=True)
        yield
        run = get_run()
        run_in = run if present is None else jnp.where(present, run, -1e30)
        att = jnp.exp(log_beta + tail + run_in)
        if diagonal:
            att = jnp.where(strict, att, 0.0)
        if present is not None:
            total = jnp.where(present, total, 0.0)
        out[g] = (run + total, _dot(att, vs_ref[sl, :]))
        yield

    def lockstep(gens):
        while gens:
            gens = [t for t in gens if next(t, "done") != "done"]

    def live(runs):
        top = functools.reduce(jnp.maximum, [jnp.max(r) for r in runs])
        return (top > -SB_DEAD).astype(jnp.int32)

    zero = jnp.zeros((gr, 1), F32)
    d_out, p_out = {}, {}
    lockstep([group(g, i, lambda: zero, True, d_out) for g in range(ngroups)]
             + [group(g, jnp.maximum(i - 1, 0), lambda g=g: d_out[g][0], False, p_out, i > 0)
                for g in range(ngroups)])
    runs = [p_out[g][0] for g in range(ngroups)]
    accs = [d_out[g][1] + p_out[g][1] for g in range(ngroups)]

    def cond(c):
        jj, _, _, go = c
        return jnp.logical_and(jj <= i, go > 0)

    def body(c):
        jj, runs, accs, _ = c
        out = {}
        lockstep([group(g, i - jj, lambda g=g: runs[g], False, out) for g in range(ngroups)])
        runs = [out[g][0] for g in range(ngroups)]
        return jj + 1, runs, [accs[g] + out[g][1] for g in range(ngroups)], live(runs)

    _, _, accs, _ = lax.while_loop(cond, body, (jnp.int32(2), runs, accs, live(runs)))
    acc = jnp.concatenate(accs, axis=0)
    o_ref[0] = (jnp.where(head0, acc[:blk], acc[blk:]) * g_ref[0]).astype(BF16)


def _sb_layer(x, norm_g, p):
    w_in, q_norm, k_norm, w_out = p
    b, seq, d = x.shape
    width = w_out.shape[0]
    x2d = x.reshape(b * seq, d)
    q, k, v, g = _sb_in(x2d, seq, norm_g, w_in, q_norm, k_norm, width)
    sh = lambda t: t.reshape(b, seq, width)
    full = pl.BlockSpec((1, seq, LANES), lambda bi, pi, qi: (bi, 0, pi))
    blk = _row_tile(seq, SB_BLOCK)
    qblk = pl.BlockSpec((1, blk, LANES), lambda bi, pi, qi: (bi, qi, pi))
    o = pl.pallas_call(
        _sb_kernel,
        grid=(b, width // LANES, seq // blk),
        in_specs=[qblk, full, full, qblk],
        out_specs=qblk,
        out_shape=jax.ShapeDtypeStruct((b, seq, width), BF16),
        compiler_params=_cparams("parallel", "parallel", "parallel"),
        name="sb_attn",
    )(sh(q), sh(k), sh(v), sh(g))
    out = _out_proj(x2d, o.reshape(b * seq, width), w_out, seq)
    return out.reshape(b, seq, d)


def kernel(x, l0_norm, l0_mu, l0_w_in, l0_w_up, l0_w0, l0_a_up, l0_a0, l0_k_k, l0_k_a, l0_r_k, l0_gn_w, l0_gn_b, l0_w_out, l1_norm, l1_w_in, l1_conv_w, l1_conv_b, l1_dt_bias, l1_a_log, l1_d_skip, l1_gnorm_w, l1_w_out, l2_norm, l2_w_in, l2_q_norm, l2_k_norm, l2_w_out, l3_norm, l3_mu, l3_w_in, l3_w_up, l3_w0, l3_a_up, l3_a0, l3_k_k, l3_k_a, l3_r_k, l3_gn_w, l3_gn_b, l3_w_out, l3_v_up, l3_v0):
    p0 = (l0_mu, l0_w_in, l0_w_up, l0_w0, l0_a_up, l0_a0, l0_k_k, l0_k_a, l0_r_k,
          l0_gn_w, l0_gn_b, l0_w_out)
    p3 = (l3_mu, l3_w_in, l3_w_up, l3_w0, l3_a_up, l3_a0, l3_k_k, l3_k_a, l3_r_k,
          l3_gn_w, l3_gn_b, l3_w_out, l3_v_up, l3_v0)
    p1 = (l1_w_in, l1_conv_w, l1_conv_b, l1_dt_bias, l1_a_log, l1_d_skip, l1_gnorm_w, l1_w_out)
    p2 = (l2_w_in, l2_q_norm, l2_k_norm, l2_w_out)
    x, v_first = _rwkv_layer(x, l0_norm, p0, None)
    x = _mamba_layer(x, l1_norm, p1)
    x = _sb_layer(x, l2_norm, p2)
    x, _ = _rwkv_layer(x, l3_norm, p3, v_first)
    return x
```

```python
import functools
import math

import jax
import jax.numpy as jnp
from jax import lax
from jax.experimental import pallas as pl
from jax.experimental.pallas import tpu as pltpu

F32 = jnp.float32
BF16 = jnp.bfloat16

LANES = 128
HEAD = 64
NORM_EPS = 1e-6
RW_GN_EPS = 64e-5
PROJ_ROWS = 512
OUT_ROWS = 1024
RW_CHUNK = 64
RW_TILES = 8
RW_BATCH = 4
RW_TIME_BLOCK = 128
SSD_CHUNK = 128
SSD_STATE = 128
SSD_GROUPS = 8
SSD_CONV = 4
SB_BLOCK = 256
SB_ROWS = 128
SB_DEAD = 104.0
VMEM_LIMIT = 56 * 1024 * 1024


def _cparams(*sem):
    return pltpu.CompilerParams(dimension_semantics=sem, vmem_limit_bytes=VMEM_LIMIT)


def _dot(a, b):
    return jnp.dot(a.astype(BF16), b.astype(BF16), preferred_element_type=F32)


def _dot_nt(a, b):
    return lax.dot_general(a.astype(BF16), b.astype(BF16), (((1,), (1,)), ((), ())),
                           preferred_element_type=F32)


def _dot_tn(a, b):
    return lax.dot_general(a.astype(BF16), b.astype(BF16), (((0,), (0,)), ((), ())),
                           preferred_element_type=F32)


def _split(x, terms):
    out = []
    rem = x
    for i in range(terms):
        p = rem.astype(BF16)
        out.append(p)
        if i + 1 < terms:
            rem = rem - p.astype(F32)
    return out


def _dot_sel_l(sel_bf16, x, terms=3):
    acc = None
    for p in _split(x, terms):
        d = jnp.dot(sel_bf16, p, preferred_element_type=F32)
        acc = d if acc is None else acc + d
    return acc


def _dot_sel_r(x, sel_bf16, terms=3):
    acc = None
    for p in _split(x, terms):
        d = jnp.dot(p, sel_bf16, preferred_element_type=F32)
        acc = d if acc is None else acc + d
    return acc


def _dot3(a, b):
    a_hi, a_lo = _split(a, 2)
    b_hi, b_lo = _split(b, 2)
    return (jnp.dot(a_hi, b_hi, preferred_element_type=F32)
            + jnp.dot(a_hi, b_lo, preferred_element_type=F32)
            + jnp.dot(a_lo, b_hi, preferred_element_type=F32))


def _rms(x, g):
    return x * lax.rsqrt(jnp.mean(x * x, axis=-1, keepdims=True) + NORM_EPS) * g


def _sigmoid(x):
    return 1.0 / (1.0 + jnp.exp(-x))


def _silu(x):
    return x * _sigmoid(x)


def _row_tile(t, target):
    tm = min(t, target)
    while t % tm:
        tm //= 2
    return tm


def _const_spec(shape):
    nd = len(shape)
    return pl.BlockSpec(shape, lambda *_: (0,) * nd, pipeline_mode=pl.Buffered(1))


def _rwkv_in_kernel(tiles_per_seq, has_vres, x_ref, xp_ref, g_ref, mu_ref, w4_ref, ws_ref,
                    wup_ref, w0_ref, aup_ref, a0_ref, *rest):
    if has_vres:
        wv_ref, vup_ref, v0_ref, vf_ref, r_o, w_o, k_o, v_o, a_o, g_o = rest
    else:
        r_o, w_o, k_o, v_o, a_o, g_o = rest
    width = r_o.shape[-1]
    g = g_ref[...]
    h = _rms(x_ref[...], g)
    hp = _rms(xp_ref[...], g)[7:8, :]
    first = (pl.program_id(0) % tiles_per_seq) == 0
    hp = jnp.where(first, 0.0, hp)
    rows = lax.broadcasted_iota(jnp.int32, h.shape, 0)
    h_prev = jnp.where(rows == 0, hp, pltpu.roll(h, 1, 0))
    dx = h_prev - h
    mu = mu_ref[...]

    def mix(i):
        return (h + dx * mu[i:i + 1, :]).astype(BF16)

    w_lo = jnp.tanh(jnp.dot(mix(1), ws_ref[...], preferred_element_type=F32))
    a_lo = jnp.dot(mix(4), ws_ref[...], preferred_element_type=F32)
    w_raw = w0_ref[...] + _dot(w_lo, wup_ref[...])
    w_o[...] = -_sigmoid(w_raw) * math.exp(-0.5)
    a_o[...] = _sigmoid(a0_ref[...] + _dot(a_lo, aup_ref[...]))
    g_o[...] = _silu(jnp.dot(mix(5), w4_ref[:, 3 * width:4 * width], preferred_element_type=F32))
    x_v = mix(3)
    v = jnp.dot(x_v, w4_ref[:, 2 * width:3 * width], preferred_element_type=F32)
    if has_vres:
        v_lo = jnp.dot(x_v, wv_ref[...], preferred_element_type=F32)
        gate = _sigmoid(v0_ref[...] + _dot(v_lo, vup_ref[...]))
        v = v + (vf_ref[...] - v) * gate
    v_o[...] = v
    r_o[...] = jnp.dot(mix(0), w4_ref[:, 0:width], preferred_element_type=F32)
    k_o[...] = jnp.dot(mix(2), w4_ref[:, width:2 * width], preferred_element_type=F32)


def _rwkv_in(x2d, seq, norm_g, mu, w_in, w_up, w0, a_up, a0, v_up=None, v0=None, v_first=None):
    n, d = x2d.shape
    width = w_up.shape[1]
    rank_w, rank_a = w_up.shape[0], a_up.shape[0]
    has_vres = v_up is not None
    tm = _row_tile(seq, PROJ_ROWS)
    tiles_per_seq = seq // tm
    w4 = w_in[:, :4 * width].astype(BF16)
    ws = w_in[:, 4 * width:4 * width + rank_w + rank_a].astype(BF16)
    wup = jnp.concatenate([w_up, jnp.zeros((rank_a, width), F32)], 0).astype(BF16)
    aup = jnp.concatenate([jnp.zeros((rank_w, width), F32), a_up], 0).astype(BF16)
    mu8 = jnp.concatenate([mu, jnp.zeros((2, d), F32)], 0)
    row = lambda p: p.reshape(1, -1)
    tile = pl.BlockSpec((tm, d), lambda i: (i, 0))
    wtile = pl.BlockSpec((tm, width), lambda i: (i, 0))
    prev = pl.BlockSpec((8, d), lambda i: (jnp.maximum(i * (tm // 8) - 1, 0), 0))
    args = [x2d, x2d, row(norm_g), mu8, w4, ws, wup, row(w0), aup, row(a0)]
    specs = [tile, prev, _const_spec((1, d)), _const_spec((8, d)), _const_spec(w4.shape),
             _const_spec(ws.shape), _const_spec(wup.shape), _const_spec((1, width)),
             _const_spec(aup.shape), _const_spec((1, width))]
    if has_vres:
        rank_v = v_up.shape[0]
        wv = jnp.pad(w_in[:, 4 * width + rank_w + rank_a:], ((0, 0), (0, LANES - rank_v))).astype(BF16)
        vup = jnp.pad(v_up, ((0, LANES - rank_v), (0, 0))).astype(BF16)
        args += [wv, vup, row(v0), v_first]
        specs += [_const_spec(wv.shape), _const_spec(vup.shape), _const_spec((1, width)), wtile]
    out_shape = [jax.ShapeDtypeStruct((n, width), F32)] * 6
    return pl.pallas_call(
        functools.partial(_rwkv_in_kernel, tiles_per_seq, has_vres),
        grid=(n // tm,),
        in_specs=specs,
        out_specs=[wtile] * 6,
        out_shape=out_shape,
        compiler_params=_cparams("parallel"),
        name="rwkv_in",
    )(*args)


def _rwkv_scan_kernel(r_ref, w_ref, k_ref, v_ref, a_ref, g_ref, kk_p, ka_p, rk_p, gw_p, gb_p,
                      y_ref, st_ref):
    L = RW_CHUNK
    tb = r_ref.shape[1]
    tiles = r_ref.shape[2] // LANES
    n2 = 2 * L
    rows = lax.broadcasted_iota(jnp.int32, (n2, LANES), 0)
    cols = lax.broadcasted_iota(jnp.int32, (n2, LANES), 1)
    own = (rows < L) == (cols < HEAD)
    lower = rows > cols
    lower_eq = rows >= cols
    eye = (rows == cols).astype(F32)
    ri = lax.broadcasted_iota(jnp.int32, (L, L), 0)
    ci = lax.broadcasted_iota(jnp.int32, (L, L), 1)
    tril_incl = (ri >= ci).astype(BF16)

    def dup(x):
        return jnp.concatenate([x, x], axis=0)

    def stack(x):
        return jnp.where(own, dup(x), 0.0)

    @pl.when(pl.program_id(2) == 0)
    def _():
        st_ref[...] = jnp.zeros_like(st_ref)

    def one_tile(bi, p, sl, cw, lw):
        ls = slice(p * LANES, (p + 1) * LANES)
        k_k, k_a, r_k = kk_p[:, ls], ka_p[:, ls], rk_p[:, ls]
        r = r_ref[bi, sl, ls]
        k = k_ref[bi, sl, ls]
        v = v_ref[bi, sl, ls]
        a = a_ref[bi, sl, ls]
        cw_last = cw[L - 1:L, :]
        e_in = jnp.exp(cw)
        e_ex = jnp.exp(cw - lw)
        e_inv = jnp.exp(-cw)
        e_end = jnp.exp(cw_last - cw)
        kk_bd = stack(k * k_k)
        nrm = jnp.sqrt(jnp.sum(kk_bd * kk_bd, axis=-1, keepdims=True))
        kk_bd = kk_bd * (1.0 / jnp.maximum(nrm, 1e-12))
        k_mod = k * (1.0 + (a - 1.0) * k_a)
        at_bd = (-kk_bd * dup(e_ex)).astype(BF16)
        bt_bd = (kk_bd * dup(a * e_inv)).astype(BF16)
        kt_bd = stack(k_mod * e_inv).astype(BF16)
        rt_bd = stack(r * e_in).astype(BF16)
        v_bd = stack(v)
        v_bf = v_bd.astype(BF16)
        bw_bd = (kk_bd * dup(a * e_end)).astype(BF16)
        kw_bd = stack(k_mod * e_end).astype(BF16)

        m = _dot_nt(jnp.concatenate([at_bd, rt_bd], axis=0),
                    jnp.concatenate([bt_bd, kt_bd], axis=0))
        yield
        a_ab = jnp.where(lower, m[:n2, :n2], 0.0)
        a_ak = jnp.where(lower, m[:n2, n2:], 0.0).astype(BF16)
        a_r = jnp.where(jnp.concatenate([lower_eq, lower_eq], axis=1), m[n2:, :], 0.0).astype(BF16)

        x = eye + a_ab
        pw = a_ab.astype(BF16)
        pw = _dot(pw, pw)
        yield
        pw = pw.astype(BF16)
        steps = int(math.log2(L)) - 1
        for s in range(steps):
            if s + 1 < steps:
                px = _dot(pw, jnp.concatenate([pw, x.astype(BF16)], axis=1))
                yield
                pw = px[:, :n2].astype(BF16)
                x = x + px[:, n2:]
            else:
                px = _dot(pw, x)
                yield
                x = x + px

        st = st_ref[bi * tiles + p]
        rhs = _dot_nt(at_bd, st) + _dot(a_ak, v_bf)
        y0 = _dot_nt(rt_bd, st)
        yield
        u = _dot(x, rhs)
        yield
        uv = jnp.concatenate([u.astype(BF16), v_bf], axis=0)
        y = y0 + _dot(a_r, uv)
        st_ref[bi * tiles + p] = st * jnp.exp(cw_last) + _dot_tn(uv, jnp.concatenate([bw_bd, kw_bd], axis=0))
        yield

        gw_bd = stack(jnp.broadcast_to(gw_p[:, ls], (L, LANES)))
        gb_bd = stack(jnp.broadcast_to(gb_p[:, ls], (L, LANES)))
        mean = jnp.sum(y, axis=-1, keepdims=True) * (1.0 / HEAD)
        d = jnp.where(own, y - mean, 0.0)
        var = jnp.sum(d * d, axis=-1, keepdims=True) * (1.0 / HEAD)
        yn = d * lax.rsqrt(var + RW_GN_EPS) * gw_bd + gb_bd
        bonus = jnp.sum(stack(r * k_mod * r_k), axis=-1, keepdims=True)
        yo = yn + bonus * v_bd
        y_ref[bi, sl, ls] = ((yo[:L] + yo[L:]) * g_ref[bi, sl, ls]).astype(BF16)

    def chunk(c, carry):
        sl = pl.ds(pl.multiple_of(c * L, L), L)
        running = []
        for bi in range(r_ref.shape[0]):
            lw = w_ref[bi, sl, :]
            cw = _dot_sel_l(tril_incl, lw)
            running += [one_tile(bi, p, sl, cw[:, p * LANES:(p + 1) * LANES],
                                 lw[:, p * LANES:(p + 1) * LANES]) for p in range(tiles)]
        while running:
            running = [t for t in running if next(t, "done") != "done"]
        return carry

    lax.fori_loop(0, tb // L, chunk, 0)


def _rwkv_scan(r, w, k, v, a, g, k_k, k_a, r_k, gn_w, gn_b):
    b, seq, width = r.shape
    gl = RW_TILES * LANES
    tb = _row_tile(seq, RW_TIME_BLOCK)
    nb = RW_BATCH if b % RW_BATCH == 0 else 1
    blk = pl.BlockSpec((nb, tb, gl), lambda i, j, t: (i, t, j))
    par = pl.BlockSpec((1, gl), lambda i, j, t: (0, j))
    row = lambda p: p.reshape(1, -1)
    return pl.pallas_call(
        _rwkv_scan_kernel,
        grid=(b // nb, width // gl, seq // tb),
        in_specs=[blk] * 6 + [par] * 5,
        out_specs=blk,
        out_shape=jax.ShapeDtypeStruct((b, seq, width), BF16),
        scratch_shapes=[pltpu.VMEM((nb * RW_TILES, LANES, LANES), F32)],
        compiler_params=_cparams("parallel", "parallel", "arbitrary"),
        name="rwkv_scan",
    )(r, w, k, v, a, g, row(k_k), row(k_a), row(r_k), row(gn_w), row(gn_b))


def _out_proj_kernel(x_ref, yg_ref, w_ref, o_ref):
    o_ref[...] = x_ref[...] + jnp.dot(yg_ref[...], w_ref[...], preferred_element_type=F32)


def _out_proj(x2d, yg2d, w_out, seq):
    n, d = x2d.shape
    width = yg2d.shape[1]
    tm = _row_tile(seq, OUT_ROWS)
    return pl.pallas_call(
        _out_proj_kernel,
        grid=(n // tm,),
        in_specs=[pl.BlockSpec((tm, d), lambda i: (i, 0)),
                  pl.BlockSpec((tm, width), lambda i: (i, 0)),
                  _const_spec((width, d))],
        out_specs=pl.BlockSpec((tm, d), lambda i: (i, 0)),
        out_shape=jax.ShapeDtypeStruct((n, d), F32),
        compiler_params=_cparams("parallel"),
        name="out_proj",
    )(x2d, yg2d, w_out.astype(BF16))


def _rwkv_layer(x, norm_g, p, v_first):
    mu, w_in, w_up, w0, a_up, a0, k_k, k_a, r_k, gn_w, gn_b, w_out = p[:12]
    b, seq, d = x.shape
    x2d = x.reshape(b * seq, d)
    if v_first is None:
        r, w, k, v, a, g = _rwkv_in(x2d, seq, norm_g, mu, w_in, w_up, w0, a_up, a0)
        v_first = v
    else:
        v_up, v0 = p[12:]
        r, w, k, v, a, g = _rwkv_in(x2d, seq, norm_g, mu, w_in, w_up, w0, a_up, a0,
                                    v_up, v0, v_first)
    width = r.shape[1]
    sh = lambda t: t.reshape(b, seq, width)
    yg = _rwkv_scan(sh(r), sh(w), sh(k), sh(v), sh(a), sh(g), k_k, k_a, r_k.reshape(-1),
                    gn_w, gn_b)
    out = _out_proj(x2d, yg.reshape(b * seq, width), w_out, seq)
    return out.reshape(b, seq, d), v_first


def _norm_proj_kernel(acts, x_ref, g_ref, *refs):
    n = len(acts)
    w_refs, o_refs = refs[:n], refs[n:]
    h = _rms(x_ref[...], g_ref[...]).astype(BF16)
    for i in sorted(range(n), key=lambda i: not acts[i]):
        y = jnp.dot(h, w_refs[i][...], preferred_element_type=F32)
        o_refs[i][...] = _silu(y) if acts[i] else y


def _norm_proj(x2d, seq, norm_g, weights, acts):
    n, d = x2d.shape
    tm = _row_tile(seq, PROJ_ROWS)
    weights = [w.astype(BF16) for w in weights]
    return pl.pallas_call(
        functools.partial(_norm_proj_kernel, tuple(acts)),
        grid=(n // tm,),
        in_specs=[pl.BlockSpec((tm, d), lambda i: (i, 0)), _const_spec((1, d))]
                 + [_const_spec(w.shape) for w in weights],
        out_specs=[pl.BlockSpec((tm, w.shape[1]), lambda i: (i, 0)) for w in weights],
        out_shape=[jax.ShapeDtypeStruct((n, w.shape[1]), F32) for w in weights],
        compiler_params=_cparams("parallel"),
        name="norm_proj",
    )(x2d, norm_g.reshape(1, d), *weights)


def _sb_in_kernel(x_ref, g_ref, wq_ref, wk_ref, wv_ref, wg_ref, qn_ref, kn_ref,
                  q_o, k_o, v_o, g_o):
    h = _rms(x_ref[...], g_ref[...]).astype(BF16)
    tm, width = q_o.shape
    head0 = lax.broadcasted_iota(jnp.int32, (tm, LANES), 1) < HEAD

    def head_rms(y, w_row, scale, o_ref):
        for j in range(width // LANES):
            ls = slice(j * LANES, (j + 1) * LANES)
            t = y[:, ls]
            sq = t * t
            s0 = jnp.sum(jnp.where(head0, sq, 0.0), axis=-1, keepdims=True)
            s1 = jnp.sum(jnp.where(head0, 0.0, sq), axis=-1, keepdims=True)
            inv = jnp.where(head0, lax.rsqrt(s0 * (1.0 / HEAD) + NORM_EPS),
                            lax.rsqrt(s1 * (1.0 / HEAD) + NORM_EPS))
            o_ref[:, ls] = (t * inv * (w_row[:, ls] * scale)).astype(BF16)

    g_o[...] = _silu(jnp.dot(h, wg_ref[...], preferred_element_type=F32))
    head_rms(jnp.dot(h, wq_ref[...], preferred_element_type=F32), qn_ref[...],
             1.0 / math.sqrt(HEAD), q_o)
    head_rms(jnp.dot(h, wk_ref[...], preferred_element_type=F32), kn_ref[...], 1.0, k_o)
    v_o[...] = jnp.dot(h, wv_ref[...], preferred_element_type=F32).astype(BF16)


def _sb_in(x2d, seq, norm_g, w_in, q_norm, k_norm, width):
    n, d = x2d.shape
    tm = _row_tile(seq, PROJ_ROWS)
    ws = [w_in[:, i * width:(i + 1) * width].astype(BF16) for i in range(4)]
    tile_w = lambda t: jnp.tile(t, width // HEAD).reshape(1, width)
    out_spec = pl.BlockSpec((tm, width), lambda i: (i, 0))
    return pl.pallas_call(
        _sb_in_kernel,
        grid=(n // tm,),
        in_specs=[pl.BlockSpec((tm, d), lambda i: (i, 0)), _const_spec((1, d))]
                 + [_const_spec((d, width))] * 4 + [_const_spec((1, width))] * 2,
        out_specs=[out_spec] * 4,
        out_shape=[jax.ShapeDtypeStruct((n, width), BF16)] * 3
                  + [jax.ShapeDtypeStruct((n, width), F32)],
        compiler_params=_cparams("parallel"),
        name="sb_in",
    )(x2d, norm_g.reshape(1, d), *ws, tile_w(q_norm), tile_w(k_norm))


def _ssd_kernel(x_ref, z_ref, u_ref, dt_ref, cw_ref, cb_ref, dtb_ref, alog_ref, dsk_ref,
                gnw_ref, exp_ref, wout_ref, o_ref, ext_ref, h_ref, y_ref):
    L = SSD_CHUNK
    width = z_ref.shape[-1]
    heads_per_group = width // HEAD // SSD_GROUPS
    gw = heads_per_group * HEAD
    c = pl.program_id(1)

    @pl.when(c == 0)
    def _():
        ext_ref[...] = jnp.zeros_like(ext_ref)
        h_ref[...] = jnp.zeros_like(h_ref)

    u = u_ref[0]
    cdim = u.shape[1]
    u3 = u.reshape(L // 8, 8, cdim)
    tail = ext_ref[...]
    sub = lax.broadcasted_iota(jnp.int32, (L // 8, 8, cdim), 1)
    conv = cb_ref[...] + cw_ref[SSD_CONV - 1:SSD_CONV, :] * u
    for j in range(SSD_CONV - 1):
        s = SSD_CONV - 1 - j
        rot = pltpu.roll(u3, s, 1)
        before = jnp.concatenate([pltpu.roll(tail, s, 0)[None], rot[:-1]], axis=0)
        shifted = jnp.where(sub < s, before, rot).reshape(L, cdim)
        conv = conv + cw_ref[j:j + 1, :] * shifted
    ext_ref[...] = u[L - 8:, :]
    xbc = _silu(conv)
    xs = xbc[:, :width]
    n_bc = SSD_GROUPS * SSD_STATE
    bm = xbc[:, width:width + n_bc]
    cm = xbc[:, width + n_bc:]

    x_in = dt_ref[0] + dtb_ref[...]
    dt = jnp.maximum(x_in, 0.0) + jnp.log1p(jnp.exp(-jnp.abs(x_in)))
    a = dt * (-jnp.exp(alog_ref[...]))
    ri = lax.broadcasted_iota(jnp.int32, (L, L), 0)
    ci = lax.broadcasted_iota(jnp.int32, (L, L), 1)
    causal = ri >= ci
    ac = _dot_sel_l(causal.astype(BF16), a)
    ac_t = ac.T
    expand = exp_ref[...]
    dt_w = _dot_sel_r(dt, expand)
    ac_w = _dot_sel_r(ac, expand)
    ac_last_w = ac_w[L - 1:L, :]
    x_dt = xs * dt_w
    e_out = jnp.exp(ac_w)
    xd = x_dt * jnp.exp(ac_last_w - ac_w)
    e_last = jnp.exp(ac_last_w)
    lane_g = lax.broadcasted_iota(jnp.int32, (L, gw), 1) // HEAD

    def one_group(g):
        bg = bm[:, g * SSD_STATE:(g + 1) * SSD_STATE].astype(BF16)
        cg = cm[:, g * SSD_STATE:(g + 1) * SSD_STATE].astype(BF16)
        gs = slice(g * gw, (g + 1) * gw)
        cb = _dot_nt(cg, bg)
        h_prev = h_ref[g]
        y_off = _dot(cg, h_prev)
        h_new = _dot_tn(bg, xd[:, gs])
        yield
        x_g = x_dt[:, gs].astype(BF16)
        lms, xms = [], []
        for r in range(heads_per_group):
            hd = g * heads_per_group + r
            seg = ac[:, hd:hd + 1] - ac_t[hd:hd + 1, :]
            lms.append((cb * jnp.exp(jnp.where(causal, seg, -1e30))).astype(BF16))
            xms.append(jnp.where(lane_g == r, x_g, jnp.zeros_like(x_g)))
        y_diag = jnp.dot(jnp.concatenate(lms, axis=1), jnp.concatenate(xms, axis=0),
                         preferred_element_type=F32)
        h_ref[g] = h_prev * e_last[:, gs] + h_new
        yield
        y_ref[:, gs] = y_off * e_out[:, gs] + y_diag

    running = [one_group(g) for g in range(SSD_GROUPS)]
    while running:
        running = [t for t in running if next(t, "done") != "done"]

    y = (y_ref[...] + xs * dsk_ref[...]) * _silu(z_ref[0])
    for g in range(SSD_GROUPS):
        gs = slice(g * gw, (g + 1) * gw)
        yg = y[:, gs]
        ms = jnp.mean(yg * yg, axis=-1, keepdims=True)
        y_ref[:, gs] = yg * lax.rsqrt(ms + NORM_EPS)
    yn = (y_ref[...] * gnw_ref[...]).astype(BF16)
    o_ref[0] = x_ref[0] + jnp.dot(yn, wout_ref[...], preferred_element_type=F32)


def _mamba_layer(x, norm_g, p):
    w_in, conv_w, conv_b, dt_bias, a_log, d_skip, gnorm_w, w_out = p
    b, seq, d = x.shape
    width = w_out.shape[0]
    heads = d_skip.shape[0]
    conv_dim = conv_w.shape[1]
    L = SSD_CHUNK
    x2d = x.reshape(b * seq, d)
    w_dt = jnp.pad(w_in[:, width + conv_dim:], ((0, 0), (0, LANES - heads)))
    z, u, dt = _norm_proj(x2d, seq, norm_g,
                          [w_in[:, :width], w_in[:, width:width + conv_dim], w_dt],
                          [False, False, False])
    pad_h = lambda v: jnp.pad(v, (0, LANES - heads)).reshape(1, LANES)
    expand = (jnp.arange(LANES)[:, None] == (jnp.arange(width) // HEAD)[None, :]).astype(BF16)
    cw8 = jnp.pad(conv_w, ((0, 8 - conv_w.shape[0]), (0, 0)))
    gw = width // SSD_GROUPS
    blk = lambda n: pl.BlockSpec((1, L, n), lambda i, j: (i, j, 0))
    out = pl.pallas_call(
        _ssd_kernel,
        grid=(b, seq // L),
        in_specs=[blk(d), blk(width), blk(conv_dim), blk(LANES),
                  _const_spec((8, conv_dim)), _const_spec((1, conv_dim)),
                  _const_spec((1, LANES)), _const_spec((1, LANES)),
                  _const_spec((1, width)), _const_spec((1, width)),
                  _const_spec((LANES, width)), _const_spec((width, d))],
        out_specs=blk(d),
        out_shape=jax.ShapeDtypeStruct((b, seq, d), F32),
        scratch_shapes=[pltpu.VMEM((8, conv_dim), F32),
                        pltpu.VMEM((SSD_GROUPS, SSD_STATE, gw), F32),
                        pltpu.VMEM((L, width), F32)],
        compiler_params=_cparams("parallel", "arbitrary"),
        name="ssd",
    )(x, z.reshape(b, seq, width), u.reshape(b, seq, conv_dim), dt.reshape(b, seq, LANES),
      cw8, conv_b.reshape(1, -1), pad_h(dt_bias), pad_h(a_log),
      jnp.repeat(d_skip, HEAD).reshape(1, -1), gnorm_w.reshape(1, -1), expand,
      w_out.astype(BF16))
    return out


def _sb_kernel(blk, q_ref, ks_ref, vs_ref, g_ref, o_ref):
    def q_block(i, carry):
        rows = pl.ds(pl.multiple_of(i * blk, blk), blk)
        _sb_block(i, q_ref.at[:, rows], ks_ref, vs_ref, g_ref.at[:, rows], o_ref.at[:, rows])
        return carry
    lax.fori_loop(0, q_ref.shape[1] // blk, q_block, 0)


def _sb_block(i, q_ref, ks_ref, vs_ref, g_ref, o_ref):
    blk = q_ref.shape[1]
    lane = lax.broadcasted_iota(jnp.int32, (blk, LANES), 1)
    head0 = lane < HEAD
    ks_ref, vs_ref = ks_ref.at[0], vs_ref.at[0]
    q = q_ref[0]
    zq = jnp.zeros_like(q)
    q_bd = jnp.concatenate([jnp.where(head0, q, zq), jnp.where(head0, zq, q)], axis=0)
    gr = min(SB_ROWS, blk)
    ngroups = 2 * blk // gr
    ji = lax.broadcasted_iota(jnp.int32, (blk, blk), 0)
    si = lax.broadcasted_iota(jnp.int32, (blk, blk), 1)
    suffix = (ji > si).astype(BF16)

    def group(g, j, get_run, diagonal, out, present=None):
        first = (g * gr) % blk
        nk = first + gr if diagonal else blk
        sl = pl.ds(pl.multiple_of(j * blk, blk), nk)
        z = _dot_nt(q_bd[g * gr:(g + 1) * gr], ks_ref[sl, :])
        yield
        log_beta = jnp.minimum(z, 0.0) - jnp.log(1.0 + jnp.exp(-jnp.abs(z)))
        log_1m = log_beta - z
        if diagonal:
            strict = (lax.broadcasted_iota(jnp.int32, (gr, nk), 0) + first
                      > lax.broadcasted_iota(jnp.int32, (gr, nk), 1))
            log_1m = jnp.where(strict, log_1m, 0.0)
        tail = _dot_sel_r(log_1m, suffix[:nk, :nk], terms=2)
        total = jnp.sum(log_1m, axis=-1, keepdims=True)
        yield
        run = get_run()
        run_in = run if present is None else jnp.where(present, run, -1e30)
        att = jnp.exp(log_beta + tail + run_in)
        if diagonal:
            att = jnp.where(strict, att, 0.0)
        if present is not None:
            total = jnp.where(present, total, 0.0)
        out[g] = (run + total, _dot(att, vs_ref[sl, :]))
        yield

    def lockstep(gens):
        while gens:
            gens = [t for t in gens if next(t, "done") != "done"]

    def live(runs):
        top = functools.reduce(jnp.maximum, [jnp.max(r) for r in runs])
        return (top > -SB_DEAD).astype(jnp.int32)

    zero = jnp.zeros((gr, 1), F32)
    d_out, p_out = {}, {}
    lockstep([group(g, i, lambda: zero, True, d_out) for g in range(ngroups)]
             + [group(g, jnp.maximum(i - 1, 0), lambda g=g: d_out[g][0], False, p_out, i > 0)
                for g in range(ngroups)])
    runs = [p_out[g][0] for g in range(ngroups)]
    accs = [d_out[g][1] + p_out[g][1] for g in range(ngroups)]

    def cond(c):
        jj, _, _, go = c
        return jnp.logical_and(jj <= i, go > 0)

    def body(c):
        jj, runs, accs, _ = c
        out = {}
        lockstep([group(g, i - jj, lambda g=g: runs[g], False, out) for g in range(ngroups)])
        runs = [out[g][0] for g in range(ngroups)]
        return jj + 1, runs, [accs[g] + out[g][1] for g in range(ngroups)], live(runs)

    _, _, accs, _ = lax.while_loop(cond, body, (jnp.int32(2), runs, accs, live(runs)))
    acc = jnp.concatenate(accs, axis=0)
    o_ref[0] = (jnp.where(head0, acc[:blk], acc[blk:]) * g_ref[0]).astype(BF16)


def _sb_layer(x, norm_g, p):
    w_in, q_norm, k_norm, w_out = p
    b, seq, d = x.shape
    width = w_out.shape[0]
    x2d = x.reshape(b * seq, d)
    q, k, v, g = _sb_in(x2d, seq, norm_g, w_in, q_norm, k_norm, width)
    sh = lambda t: t.reshape(b, seq, width)
    full = pl.BlockSpec((1, seq, LANES), lambda bi, pi: (bi, 0, pi))
    o = pl.pallas_call(
        functools.partial(_sb_kernel, _row_tile(seq, SB_BLOCK)),
        grid=(b, width // LANES),
        in_specs=[full] * 4,
        out_specs=full,
        out_shape=jax.ShapeDtypeStruct((b, seq, width), BF16),
        compiler_params=_cparams("parallel", "parallel"),
        name="sb_attn",
    )(sh(q), sh(k), sh(v), sh(g))
    out = _out_proj(x2d, o.reshape(b * seq, width), w_out, seq)
    return out.reshape(b, seq, d)


def kernel(x, l0_norm, l0_mu, l0_w_in, l0_w_up, l0_w0, l0_a_up, l0_a0, l0_k_k, l0_k_a, l0_r_k, l0_gn_w, l0_gn_b, l0_w_out, l1_norm, l1_w_in, l1_conv_w, l1_conv_b, l1_dt_bias, l1_a_log, l1_d_skip, l1_gnorm_w, l1_w_out, l2_norm, l2_w_in, l2_q_norm, l2_k_norm, l2_w_out, l3_norm, l3_mu, l3_w_in, l3_w_up, l3_w0, l3_a_up, l3_a0, l3_k_k, l3_k_a, l3_r_k, l3_gn_w, l3_gn_b, l3_w_out, l3_v_up, l3_v0):
    p0 = (l0_mu, l0_w_in, l0_w_up, l0_w0, l0_a_up, l0_a0, l0_k_k, l0_k_a, l0_r_k,
          l0_gn_w, l0_gn_b, l0_w_out)
    p3 = (l3_mu, l3_w_in, l3_w_up, l3_w0, l3_a_up, l3_a0, l3_k_k, l3_k_a, l3_r_k,
          l3_gn_w, l3_gn_b, l3_w_out, l3_v_up, l3_v0)
    p1 = (l1_w_in, l1_conv_w, l1_conv_b, l1_dt_bias, l1_a_log, l1_d_skip, l1_gnorm_w, l1_w_out)
    p2 = (l2_w_in, l2_q_norm, l2_k_norm, l2_w_out)
    x, v_first = _rwkv_layer(x, l0_norm, p0, None)
    x = _mamba_layer(x, l1_norm, p1)
    x = _sb_layer(x, l2_norm, p2)
    x, _ = _rwkv_layer(x, l3_norm, p3, v_first)
    return x
```

```python
import functools
import math

import jax
import jax.numpy as jnp
from jax import lax
from jax.experimental import pallas as pl
from jax.experimental.pallas import tpu as pltpu

F32 = jnp.float32
BF16 = jnp.bfloat16

LANES = 128
HEAD = 64
NORM_EPS = 1e-6
RW_GN_EPS = 64e-5
PROJ_ROWS = 512
OUT_ROWS = 1024
RW_CHUNK = 64
RW_TILES = 8
RW_BATCH = 4
RW_TIME_BLOCK = 128
SSD_CHUNK = 128
SSD_STEP_CHUNKS = 4
SSD_STATE = 128
SSD_GROUPS = 8
SSD_CONV = 4
SB_BLOCK = 256
SB_ROWS = 128
SB_DEAD = 104.0
VMEM_LIMIT = 56 * 1024 * 1024


def _cparams(*sem):
    return pltpu.CompilerParams(dimension_semantics=sem, vmem_limit_bytes=VMEM_LIMIT)


def _dot(a, b):
    return jnp.dot(a.astype(BF16), b.astype(BF16), preferred_element_type=F32)


def _dot_nt(a, b):
    return lax.dot_general(a.astype(BF16), b.astype(BF16), (((1,), (1,)), ((), ())),
                           preferred_element_type=F32)


def _dot_tn(a, b):
    return lax.dot_general(a.astype(BF16), b.astype(BF16), (((0,), (0,)), ((), ())),
                           preferred_element_type=F32)


def _split(x, terms):
    out = []
    rem = x
    for i in range(terms):
        p = rem.astype(BF16)
        out.append(p)
        if i + 1 < terms:
            rem = rem - p.astype(F32)
    return out


def _dot_sel_l(sel_bf16, x, terms=3):
    acc = None
    for p in _split(x, terms):
        d = jnp.dot(sel_bf16, p, preferred_element_type=F32)
        acc = d if acc is None else acc + d
    return acc


def _dot_sel_r(x, sel_bf16, terms=3):
    acc = None
    for p in _split(x, terms):
        d = jnp.dot(p, sel_bf16, preferred_element_type=F32)
        acc = d if acc is None else acc + d
    return acc


def _rms(x, g):
    return x * lax.rsqrt(jnp.mean(x * x, axis=-1, keepdims=True) + NORM_EPS) * g


def _sigmoid(x):
    return 1.0 / (1.0 + jnp.exp(-x))


def _silu(x):
    return x * _sigmoid(x)


def _row_tile(t, target):
    tm = min(t, target)
    while t % tm:
        tm //= 2
    return tm


def _const_spec(shape):
    nd = len(shape)
    return pl.BlockSpec(shape, lambda *_: (0,) * nd, pipeline_mode=pl.Buffered(1))


def _rwkv_in_kernel(tiles_per_seq, has_vres, x_ref, xp_ref, g_ref, mu_ref, w4_ref, ws_ref,
                    wup_ref, w0_ref, aup_ref, a0_ref, *rest):
    if has_vres:
        wv_ref, vup_ref, v0_ref, vf_ref, r_o, w_o, k_o, v_o, a_o, g_o = rest
    else:
        r_o, w_o, k_o, v_o, a_o, g_o = rest
    width = r_o.shape[-1]
    g = g_ref[...]
    h = _rms(x_ref[...], g)
    hp = _rms(xp_ref[...], g)[7:8, :]
    first = (pl.program_id(0) % tiles_per_seq) == 0
    hp = jnp.where(first, 0.0, hp)
    rows = lax.broadcasted_iota(jnp.int32, h.shape, 0)
    h_prev = jnp.where(rows == 0, hp, pltpu.roll(h, 1, 0))
    dx = h_prev - h
    mu = mu_ref[...]

    def mix(i):
        return (h + dx * mu[i:i + 1, :]).astype(BF16)

    w_lo = jnp.tanh(jnp.dot(mix(1), ws_ref[...], preferred_element_type=F32))
    a_lo = jnp.dot(mix(4), ws_ref[...], preferred_element_type=F32)
    w_raw = w0_ref[...] + _dot(w_lo, wup_ref[...])
    w_o[...] = -_sigmoid(w_raw) * math.exp(-0.5)
    a_o[...] = _sigmoid(a0_ref[...] + _dot(a_lo, aup_ref[...]))
    g_o[...] = _silu(jnp.dot(mix(5), w4_ref[:, 3 * width:4 * width], preferred_element_type=F32))
    x_v = mix(3)
    v = jnp.dot(x_v, w4_ref[:, 2 * width:3 * width], preferred_element_type=F32)
    if has_vres:
        v_lo = jnp.dot(x_v, wv_ref[...], preferred_element_type=F32)
        gate = _sigmoid(v0_ref[...] + _dot(v_lo, vup_ref[...]))
        v = v + (vf_ref[...] - v) * gate
    v_o[...] = v
    r_o[...] = jnp.dot(mix(0), w4_ref[:, 0:width], preferred_element_type=F32)
    k_o[...] = jnp.dot(mix(2), w4_ref[:, width:2 * width], preferred_element_type=F32)


def _rwkv_in(x2d, seq, norm_g, mu, w_in, w_up, w0, a_up, a0, v_up=None, v0=None, v_first=None):
    n, d = x2d.shape
    width = w_up.shape[1]
    rank_w, rank_a = w_up.shape[0], a_up.shape[0]
    has_vres = v_up is not None
    tm = _row_tile(seq, PROJ_ROWS)
    tiles_per_seq = seq // tm
    w4 = w_in[:, :4 * width].astype(BF16)
    ws = w_in[:, 4 * width:4 * width + rank_w + rank_a].astype(BF16)
    wup = jnp.concatenate([w_up, jnp.zeros((rank_a, width), F32)], 0).astype(BF16)
    aup = jnp.concatenate([jnp.zeros((rank_w, width), F32), a_up], 0).astype(BF16)
    mu8 = jnp.concatenate([mu, jnp.zeros((2, d), F32)], 0)
    row = lambda p: p.reshape(1, -1)
    tile = pl.BlockSpec((tm, d), lambda i: (i, 0))
    wtile = pl.BlockSpec((tm, width), lambda i: (i, 0))
    prev = pl.BlockSpec((8, d), lambda i: (jnp.maximum(i * (tm // 8) - 1, 0), 0))
    args = [x2d, x2d, row(norm_g), mu8, w4, ws, wup, row(w0), aup, row(a0)]
    specs = [tile, prev, _const_spec((1, d)), _const_spec((8, d)), _const_spec(w4.shape),
             _const_spec(ws.shape), _const_spec(wup.shape), _const_spec((1, width)),
             _const_spec(aup.shape), _const_spec((1, width))]
    if has_vres:
        rank_v = v_up.shape[0]
        wv = jnp.pad(w_in[:, 4 * width + rank_w + rank_a:], ((0, 0), (0, LANES - rank_v))).astype(BF16)
        vup = jnp.pad(v_up, ((0, LANES - rank_v), (0, 0))).astype(BF16)
        args += [wv, vup, row(v0), v_first]
        specs += [_const_spec(wv.shape), _const_spec(vup.shape), _const_spec((1, width)), wtile]
    out_shape = [jax.ShapeDtypeStruct((n, width), F32)] * 6
    return pl.pallas_call(
        functools.partial(_rwkv_in_kernel, tiles_per_seq, has_vres),
        grid=(n // tm,),
        in_specs=specs,
        out_specs=[wtile] * 6,
        out_shape=out_shape,
        compiler_params=_cparams("parallel"),
        name="rwkv_in",
    )(*args)


def _rwkv_scan_kernel(r_ref, w_ref, k_ref, v_ref, a_ref, g_ref, kk_p, ka_p, rk_p, gw_p, gb_p,
                      y_ref, st_ref):
    L = RW_CHUNK
    tb = r_ref.shape[1]
    tiles = r_ref.shape[2] // LANES
    n2 = 2 * L
    rows = lax.broadcasted_iota(jnp.int32, (n2, LANES), 0)
    cols = lax.broadcasted_iota(jnp.int32, (n2, LANES), 1)
    own = (rows < L) == (cols < HEAD)
    lower = rows > cols
    lower_eq = rows >= cols
    eye = (rows == cols).astype(F32)
    ri = lax.broadcasted_iota(jnp.int32, (L, L), 0)
    ci = lax.broadcasted_iota(jnp.int32, (L, L), 1)
    tril_incl = (ri >= ci).astype(BF16)

    def dup(x):
        return jnp.concatenate([x, x], axis=0)

    def stack(x):
        return jnp.where(own, dup(x), 0.0)

    @pl.when(pl.program_id(2) == 0)
    def _():
        st_ref[...] = jnp.zeros_like(st_ref)

    def one_tile(bi, p, sl, cw, lw):
        ls = slice(p * LANES, (p + 1) * LANES)
        k_k, k_a, r_k = kk_p[:, ls], ka_p[:, ls], rk_p[:, ls]
        r = r_ref[bi, sl, ls]
        k = k_ref[bi, sl, ls]
        v = v_ref[bi, sl, ls]
        a = a_ref[bi, sl, ls]
        cw_last = cw[L - 1:L, :]
        e_in = jnp.exp(cw)
        e_ex = jnp.exp(cw - lw)
        e_inv = jnp.exp(-cw)
        e_end = jnp.exp(cw_last - cw)
        kk_bd = stack(k * k_k)
        nrm = jnp.sqrt(jnp.sum(kk_bd * kk_bd, axis=-1, keepdims=True))
        kk_bd = kk_bd * (1.0 / jnp.maximum(nrm, 1e-12))
        k_mod = k * (1.0 + (a - 1.0) * k_a)
        at_bd = (-kk_bd * dup(e_ex)).astype(BF16)
        bt_bd = (kk_bd * dup(a * e_inv)).astype(BF16)
        kt_bd = stack(k_mod * e_inv).astype(BF16)
        rt_bd = stack(r * e_in).astype(BF16)
        v_bd = stack(v)
        v_bf = v_bd.astype(BF16)
        bw_bd = (kk_bd * dup(a * e_end)).astype(BF16)
        kw_bd = stack(k_mod * e_end).astype(BF16)

        m = _dot_nt(jnp.concatenate([at_bd, rt_bd], axis=0),
                    jnp.concatenate([bt_bd, kt_bd], axis=0))
        yield
        a_ab = jnp.where(lower, m[:n2, :n2], 0.0)
        a_ak = jnp.where(lower, m[:n2, n2:], 0.0).astype(BF16)
        a_r = jnp.where(jnp.concatenate([lower_eq, lower_eq], axis=1), m[n2:, :], 0.0).astype(BF16)

        x = eye + a_ab
        pw = a_ab.astype(BF16)
        pw = _dot(pw, pw)
        yield
        pw = pw.astype(BF16)
        steps = int(math.log2(L)) - 1
        for s in range(steps):
            if s + 1 < steps:
                px = _dot(pw, jnp.concatenate([pw, x.astype(BF16)], axis=1))
                yield
                pw = px[:, :n2].astype(BF16)
                x = x + px[:, n2:]
            else:
                px = _dot(pw, x)
                yield
                x = x + px

        st = st_ref[bi * tiles + p]
        rhs = _dot_nt(at_bd, st) + _dot(a_ak, v_bf)
        y0 = _dot_nt(rt_bd, st)
        yield
        u = _dot(x, rhs)
        yield
        uv = jnp.concatenate([u.astype(BF16), v_bf], axis=0)
        y = y0 + _dot(a_r, uv)
        st_ref[bi * tiles + p] = st * jnp.exp(cw_last) + _dot_tn(uv, jnp.concatenate([bw_bd, kw_bd], axis=0))
        yield

        gw_bd = stack(jnp.broadcast_to(gw_p[:, ls], (L, LANES)))
        gb_bd = stack(jnp.broadcast_to(gb_p[:, ls], (L, LANES)))
        mean = jnp.sum(y, axis=-1, keepdims=True) * (1.0 / HEAD)
        d = jnp.where(own, y - mean, 0.0)
        var = jnp.sum(d * d, axis=-1, keepdims=True) * (1.0 / HEAD)
        yn = d * lax.rsqrt(var + RW_GN_EPS) * gw_bd + gb_bd
        bonus = jnp.sum(stack(r * k_mod * r_k), axis=-1, keepdims=True)
        yo = yn + bonus * v_bd
        y_ref[bi, sl, ls] = ((yo[:L] + yo[L:]) * g_ref[bi, sl, ls]).astype(BF16)

    def chunk(c, carry):
        sl = pl.ds(pl.multiple_of(c * L, L), L)
        running = []
        for bi in range(r_ref.shape[0]):
            lw = w_ref[bi, sl, :]
            cw = _dot_sel_l(tril_incl, lw)
            running += [one_tile(bi, p, sl, cw[:, p * LANES:(p + 1) * LANES],
                                 lw[:, p * LANES:(p + 1) * LANES]) for p in range(tiles)]
        while running:
            running = [t for t in running if next(t, "done") != "done"]
        return carry

    lax.fori_loop(0, tb // L, chunk, 0)


def _rwkv_scan(r, w, k, v, a, g, k_k, k_a, r_k, gn_w, gn_b):
    b, seq, width = r.shape
    gl = RW_TILES * LANES
    tb = _row_tile(seq, RW_TIME_BLOCK)
    nb = RW_BATCH if b % RW_BATCH == 0 else 1
    blk = pl.BlockSpec((nb, tb, gl), lambda i, j, t: (i, t, j))
    par = pl.BlockSpec((1, gl), lambda i, j, t: (0, j))
    row = lambda p: p.reshape(1, -1)
    return pl.pallas_call(
        _rwkv_scan_kernel,
        grid=(b // nb, width // gl, seq // tb),
        in_specs=[blk] * 6 + [par] * 5,
        out_specs=blk,
        out_shape=jax.ShapeDtypeStruct((b, seq, width), BF16),
        scratch_shapes=[pltpu.VMEM((nb * RW_TILES, LANES, LANES), F32)],
        compiler_params=_cparams("parallel", "parallel", "arbitrary"),
        name="rwkv_scan",
    )(r, w, k, v, a, g, row(k_k), row(k_a), row(r_k), row(gn_w), row(gn_b))


def _out_proj_kernel(x_ref, yg_ref, w_ref, o_ref):
    o_ref[...] = x_ref[...] + jnp.dot(yg_ref[...], w_ref[...], preferred_element_type=F32)


def _out_proj(x2d, yg2d, w_out, seq):
    n, d = x2d.shape
    width = yg2d.shape[1]
    tm = _row_tile(seq, OUT_ROWS)
    return pl.pallas_call(
        _out_proj_kernel,
        grid=(n // tm,),
        in_specs=[pl.BlockSpec((tm, d), lambda i: (i, 0)),
                  pl.BlockSpec((tm, width), lambda i: (i, 0)),
                  _const_spec((width, d))],
        out_specs=pl.BlockSpec((tm, d), lambda i: (i, 0)),
        out_shape=jax.ShapeDtypeStruct((n, d), F32),
        compiler_params=_cparams("parallel"),
        name="out_proj",
    )(x2d, yg2d, w_out.astype(BF16))


def _rwkv_layer(x, norm_g, p, v_first):
    mu, w_in, w_up, w0, a_up, a0, k_k, k_a, r_k, gn_w, gn_b, w_out = p[:12]
    b, seq, d = x.shape
    x2d = x.reshape(b * seq, d)
    if v_first is None:
        r, w, k, v, a, g = _rwkv_in(x2d, seq, norm_g, mu, w_in, w_up, w0, a_up, a0)
        v_first = v
    else:
        v_up, v0 = p[12:]
        r, w, k, v, a, g = _rwkv_in(x2d, seq, norm_g, mu, w_in, w_up, w0, a_up, a0,
                                    v_up, v0, v_first)
    width = r.shape[1]
    sh = lambda t: t.reshape(b, seq, width)
    yg = _rwkv_scan(sh(r), sh(w), sh(k), sh(v), sh(a), sh(g), k_k, k_a, r_k.reshape(-1),
                    gn_w, gn_b)
    out = _out_proj(x2d, yg.reshape(b * seq, width), w_out, seq)
    return out.reshape(b, seq, d), v_first


def _norm_proj_kernel(acts, x_ref, g_ref, *refs):
    n = len(acts)
    w_refs, o_refs = refs[:n], refs[n:]
    h = _rms(x_ref[...], g_ref[...]).astype(BF16)
    for i in sorted(range(n), key=lambda i: not acts[i]):
        y = jnp.dot(h, w_refs[i][...], preferred_element_type=F32)
        o_refs[i][...] = _silu(y) if acts[i] else y


def _norm_proj(x2d, seq, norm_g, weights, acts):
    n, d = x2d.shape
    tm = _row_tile(seq, PROJ_ROWS)
    weights = [w.astype(BF16) for w in weights]
    return pl.pallas_call(
        functools.partial(_norm_proj_kernel, tuple(acts)),
        grid=(n // tm,),
        in_specs=[pl.BlockSpec((tm, d), lambda i: (i, 0)), _const_spec((1, d))]
                 + [_const_spec(w.shape) for w in weights],
        out_specs=[pl.BlockSpec((tm, w.shape[1]), lambda i: (i, 0)) for w in weights],
        out_shape=[jax.ShapeDtypeStruct((n, w.shape[1]), F32) for w in weights],
        compiler_params=_cparams("parallel"),
        name="norm_proj",
    )(x2d, norm_g.reshape(1, d), *weights)


def _sb_in_kernel(x_ref, g_ref, wq_ref, wk_ref, wv_ref, wg_ref, qn_ref, kn_ref,
                  q_o, k_o, v_o, g_o):
    h = _rms(x_ref[...], g_ref[...]).astype(BF16)
    tm, width = q_o.shape
    head0 = lax.broadcasted_iota(jnp.int32, (tm, LANES), 1) < HEAD

    def head_rms(y, w_row, scale, o_ref):
        for j in range(width // LANES):
            ls = slice(j * LANES, (j + 1) * LANES)
            t = y[:, ls]
            sq = t * t
            s0 = jnp.sum(jnp.where(head0, sq, 0.0), axis=-1, keepdims=True)
            s1 = jnp.sum(jnp.where(head0, 0.0, sq), axis=-1, keepdims=True)
            inv = jnp.where(head0, lax.rsqrt(s0 * (1.0 / HEAD) + NORM_EPS),
                            lax.rsqrt(s1 * (1.0 / HEAD) + NORM_EPS))
            o_ref[:, ls] = (t * inv * (w_row[:, ls] * scale)).astype(BF16)

    g_o[...] = _silu(jnp.dot(h, wg_ref[...], preferred_element_type=F32))
    head_rms(jnp.dot(h, wq_ref[...], preferred_element_type=F32), qn_ref[...],
             1.0 / math.sqrt(HEAD), q_o)
    head_rms(jnp.dot(h, wk_ref[...], preferred_element_type=F32), kn_ref[...], 1.0, k_o)
    v_o[...] = jnp.dot(h, wv_ref[...], preferred_element_type=F32).astype(BF16)


def _sb_in(x2d, seq, norm_g, w_in, q_norm, k_norm, width):
    n, d = x2d.shape
    tm = _row_tile(seq, PROJ_ROWS)
    ws = [w_in[:, i * width:(i + 1) * width].astype(BF16) for i in range(4)]
    tile_w = lambda t: jnp.tile(t, width // HEAD).reshape(1, width)
    out_spec = pl.BlockSpec((tm, width), lambda i: (i, 0))
    return pl.pallas_call(
        _sb_in_kernel,
        grid=(n // tm,),
        in_specs=[pl.BlockSpec((tm, d), lambda i: (i, 0)), _const_spec((1, d))]
                 + [_const_spec((d, width))] * 4 + [_const_spec((1, width))] * 2,
        out_specs=[out_spec] * 4,
        out_shape=[jax.ShapeDtypeStruct((n, width), BF16)] * 3
                  + [jax.ShapeDtypeStruct((n, width), F32)],
        compiler_params=_cparams("parallel"),
        name="sb_in",
    )(x2d, norm_g.reshape(1, d), *ws, tile_w(q_norm), tile_w(k_norm))


def _ssd_kernel(x_ref, z_ref, u_ref, dt_ref, *rest):
    *consts, o_ref, ext_ref, h_ref, y_ref = rest
    L = SSD_CHUNK

    def chunk(c, carry):
        rows = pl.ds(pl.multiple_of(c * L, L), L)
        start = jnp.logical_and(pl.program_id(1) == 0, c == 0)
        _ssd_chunk(start, x_ref.at[:, rows], z_ref.at[:, rows], u_ref.at[:, rows],
                   dt_ref.at[:, rows], *consts, o_ref.at[:, rows], ext_ref, h_ref, y_ref)
        return carry
    lax.fori_loop(0, x_ref.shape[1] // L, chunk, 0)


def _ssd_chunk(start, x_ref, z_ref, u_ref, dt_ref, cw_ref, cb_ref, dtb_ref, alog_ref, dsk_ref,
               gnw_ref, exp_ref, wout_ref, o_ref, ext_ref, h_ref, y_ref):
    L = SSD_CHUNK
    width = z_ref.shape[-1]
    heads_per_group = width // HEAD // SSD_GROUPS
    gw = heads_per_group * HEAD

    @pl.when(start)
    def _():
        ext_ref[...] = jnp.zeros_like(ext_ref)
        h_ref[...] = jnp.zeros_like(h_ref)

    u = u_ref[0]
    cdim = u.shape[1]
    u3 = u.reshape(L // 8, 8, cdim)
    tail = ext_ref[...]
    sub = lax.broadcasted_iota(jnp.int32, (L // 8, 8, cdim), 1)
    conv = cb_ref[...] + cw_ref[SSD_CONV - 1:SSD_CONV, :] * u
    for j in range(SSD_CONV - 1):
        s = SSD_CONV - 1 - j
        rot = pltpu.roll(u3, s, 1)
        before = jnp.concatenate([pltpu.roll(tail, s, 0)[None], rot[:-1]], axis=0)
        shifted = jnp.where(sub < s, before, rot).reshape(L, cdim)
        conv = conv + cw_ref[j:j + 1, :] * shifted
    ext_ref[...] = u[L - 8:, :]
    xbc = _silu(conv)
    xs = xbc[:, :width]
    n_bc = SSD_GROUPS * SSD_STATE
    bm = xbc[:, width:width + n_bc]
    cm = xbc[:, width + n_bc:]

    x_in = dt_ref[0] + dtb_ref[...]
    dt = jnp.maximum(x_in, 0.0) + jnp.log1p(jnp.exp(-jnp.abs(x_in)))
    a = dt * (-jnp.exp(alog_ref[...]))
    ri = lax.broadcasted_iota(jnp.int32, (L, L), 0)
    ci = lax.broadcasted_iota(jnp.int32, (L, L), 1)
    causal = ri >= ci
    ac = _dot_sel_l(causal.astype(BF16), a)
    ac_t = ac.T
    expand = exp_ref[...]
    dt_w = _dot_sel_r(dt, expand)
    ac_w = _dot_sel_r(ac, expand)
    ac_last_w = ac_w[L - 1:L, :]
    x_dt = xs * dt_w
    e_out = jnp.exp(ac_w)
    xd = x_dt * jnp.exp(ac_last_w - ac_w)
    e_last = jnp.exp(ac_last_w)
    lane_g = lax.broadcasted_iota(jnp.int32, (L, gw), 1) // HEAD

    def one_group(g):
        bg = bm[:, g * SSD_STATE:(g + 1) * SSD_STATE].astype(BF16)
        cg = cm[:, g * SSD_STATE:(g + 1) * SSD_STATE].astype(BF16)
        gs = slice(g * gw, (g + 1) * gw)
        cb = _dot_nt(cg, bg)
        h_prev = h_ref[g]
        y_off = _dot(cg, h_prev)
        h_new = _dot_tn(bg, xd[:, gs])
        yield
        x_g = x_dt[:, gs].astype(BF16)
        lms, xms = [], []
        for r in range(heads_per_group):
            hd = g * heads_per_group + r
            seg = ac[:, hd:hd + 1] - ac_t[hd:hd + 1, :]
            lms.append((cb * jnp.exp(jnp.where(causal, seg, -1e30))).astype(BF16))
            xms.append(jnp.where(lane_g == r, x_g, jnp.zeros_like(x_g)))
        y_diag = jnp.dot(jnp.concatenate(lms, axis=1), jnp.concatenate(xms, axis=0),
                         preferred_element_type=F32)
        h_ref[g] = h_prev * e_last[:, gs] + h_new
        yield
        y_ref[:, gs] = y_off * e_out[:, gs] + y_diag

    running = [one_group(g) for g in range(SSD_GROUPS)]
    while running:
        running = [t for t in running if next(t, "done") != "done"]

    y = (y_ref[...] + xs * dsk_ref[...]) * _silu(z_ref[0])
    for g in range(SSD_GROUPS):
        gs = slice(g * gw, (g + 1) * gw)
        yg = y[:, gs]
        ms = jnp.mean(yg * yg, axis=-1, keepdims=True)
        y_ref[:, gs] = yg * lax.rsqrt(ms + NORM_EPS)
    yn = (y_ref[...] * gnw_ref[...]).astype(BF16)
    o_ref[0] = x_ref[0] + jnp.dot(yn, wout_ref[...], preferred_element_type=F32)


def _mamba_layer(x, norm_g, p):
    w_in, conv_w, conv_b, dt_bias, a_log, d_skip, gnorm_w, w_out = p
    b, seq, d = x.shape
    width = w_out.shape[0]
    heads = d_skip.shape[0]
    conv_dim = conv_w.shape[1]
    L = SSD_CHUNK
    x2d = x.reshape(b * seq, d)
    w_dt = jnp.pad(w_in[:, width + conv_dim:], ((0, 0), (0, LANES - heads)))
    z, u, dt = _norm_proj(x2d, seq, norm_g,
                          [w_in[:, :width], w_in[:, width:width + conv_dim], w_dt],
                          [False, False, False])
    pad_h = lambda v: jnp.pad(v, (0, LANES - heads)).reshape(1, LANES)
    expand = (jnp.arange(LANES)[:, None] == (jnp.arange(width) // HEAD)[None, :]).astype(BF16)
    cw8 = jnp.pad(conv_w, ((0, 8 - conv_w.shape[0]), (0, 0)))
    gw = width // SSD_GROUPS
    step = _row_tile(seq, SSD_STEP_CHUNKS * L)
    blk = lambda n: pl.BlockSpec((1, step, n), lambda i, j: (i, j, 0))
    out = pl.pallas_call(
        _ssd_kernel,
        grid=(b, seq // step),
        in_specs=[blk(d), blk(width), blk(conv_dim), blk(LANES),
                  _const_spec((8, conv_dim)), _const_spec((1, conv_dim)),
                  _const_spec((1, LANES)), _const_spec((1, LANES)),
                  _const_spec((1, width)), _const_spec((1, width)),
                  _const_spec((LANES, width)), _const_spec((width, d))],
        out_specs=blk(d),
        out_shape=jax.ShapeDtypeStruct((b, seq, d), F32),
        scratch_shapes=[pltpu.VMEM((8, conv_dim), F32),
                        pltpu.VMEM((SSD_GROUPS, SSD_STATE, gw), F32),
                        pltpu.VMEM((L, width), F32)],
        compiler_params=_cparams("parallel", "arbitrary"),
        name="ssd",
    )(x, z.reshape(b, seq, width), u.reshape(b, seq, conv_dim), dt.reshape(b, seq, LANES),
      cw8, conv_b.reshape(1, -1), pad_h(dt_bias), pad_h(a_log),
      jnp.repeat(d_skip, HEAD).reshape(1, -1), gnorm_w.reshape(1, -1), expand,
      w_out.astype(BF16))
    return out


def _sb_kernel(blk, q_ref, ks_ref, vs_ref, g_ref, o_ref):
    def q_block(i, carry):
        rows = pl.ds(pl.multiple_of(i * blk, blk), blk)
        _sb_block(i, q_ref.at[:, rows], ks_ref, vs_ref, g_ref.at[:, rows], o_ref.at[:, rows])
        return carry
    lax.fori_loop(0, q_ref.shape[1] // blk, q_block, 0)


def _sb_block(i, q_ref, ks_ref, vs_ref, g_ref, o_ref):
    blk = q_ref.shape[1]
    lane = lax.broadcasted_iota(jnp.int32, (blk, LANES), 1)
    head0 = lane < HEAD
    ks_ref, vs_ref = ks_ref.at[0], vs_ref.at[0]
    q = q_ref[0]
    zq = jnp.zeros_like(q)
    q_bd = jnp.concatenate([jnp.where(head0, q, zq), jnp.where(head0, zq, q)], axis=0)
    gr = min(SB_ROWS, blk)
    ngroups = 2 * blk // gr
    ji = lax.broadcasted_iota(jnp.int32, (blk, blk), 0)
    si = lax.broadcasted_iota(jnp.int32, (blk, blk), 1)
    suffix = (ji > si).astype(BF16)

    def group(g, j, get_run, diagonal, out, present=None):
        first = (g * gr) % blk
        nk = first + gr if diagonal else blk
        sl = pl.ds(pl.multiple_of(j * blk, blk), nk)
        z = _dot_nt(q_bd[g * gr:(g + 1) * gr], ks_ref[sl, :])
        yield
        log_beta = jnp.minimum(z, 0.0) - jnp.log(1.0 + jnp.exp(-jnp.abs(z)))
        log_1m = log_beta - z
        if diagonal:
            strict = (lax.broadcasted_iota(jnp.int32, (gr, nk), 0) + first
                      > lax.broadcasted_iota(jnp.int32, (gr, nk), 1))
            log_1m = jnp.where(strict, log_1m, 0.0)
        tail = _dot_sel_r(log_1m, suffix[:nk, :nk], terms=2)
        total = jnp.sum(log_1m, axis=-1, keepdims=True)
        yield
        run = get_run()
        run_in = run if present is None else jnp.where(present, run, -1e30)
        att = jnp.exp(log_beta + tail + run_in)
        if diagonal:
            att = jnp.where(strict, att, 0.0)
        if present is not None:
            total = jnp.where(present, total, 0.0)
        out[g] = (run + total, _dot(att, vs_ref[sl, :]))
        yield

    def lockstep(gens):
        while gens:
            gens = [t for t in gens if next(t, "done") != "done"]

    def live(runs):
        top = functools.reduce(jnp.maximum, [jnp.max(r) for r in runs])
        return (top > -SB_DEAD).astype(jnp.int32)

    zero = jnp.zeros((gr, 1), F32)
    d_out, p_out = {}, {}
    lockstep([group(g, i, lambda: zero, True, d_out) for g in range(ngroups)]
             + [group(g, jnp.maximum(i - 1, 0), lambda g=g: d_out[g][0], False, p_out, i > 0)
                for g in range(ngroups)])
    runs = [p_out[g][0] for g in range(ngroups)]
    accs = [d_out[g][1] + p_out[g][1] for g in range(ngroups)]

    def cond(c):
        jj, _, _, go = c
        return jnp.logical_and(jj <= i, go > 0)

    def body(c):
        jj, runs, accs, _ = c
        out = {}
        lockstep([group(g, i - jj, lambda g=g: runs[g], False, out) for g in range(ngroups)])
        runs = [out[g][0] for g in range(ngroups)]
        return jj + 1, runs, [accs[g] + out[g][1] for g in range(ngroups)], live(runs)

    _, _, accs, _ = lax.while_loop(cond, body, (jnp.int32(2), runs, accs, live(runs)))
    acc = jnp.concatenate(accs, axis=0)
    o_ref[0] = (jnp.where(head0, acc[:blk], acc[blk:]) * g_ref[0]).astype(BF16)


def _sb_layer(x, norm_g, p):
    w_in, q_norm, k_norm, w_out = p
    b, seq, d = x.shape
    width = w_out.shape[0]
    x2d = x.reshape(b * seq, d)
    q, k, v, g = _sb_in(x2d, seq, norm_g, w_in, q_norm, k_norm, width)
    sh = lambda t: t.reshape(b, seq, width)
    full = pl.BlockSpec((1, seq, LANES), lambda bi, pi: (bi, 0, pi))
    o = pl.pallas_call(
        functools.partial(_sb_kernel, _row_tile(seq, SB_BLOCK)),
        grid=(b, width // LANES),
        in_specs=[full] * 4,
        out_specs=full,
        out_shape=jax.ShapeDtypeStruct((b, seq, width), BF16),
        compiler_params=_cparams("parallel", "parallel"),
        name="sb_attn",
    )(sh(q), sh(k), sh(v), sh(g))
    out = _out_proj(x2d, o.reshape(b * seq, width), w_out, seq)
    return out.reshape(b, seq, d)


def kernel(x, l0_norm, l0_mu, l0_w_in, l0_w_up, l0_w0, l0_a_up, l0_a0, l0_k_k, l0_k_a, l0_r_k, l0_gn_w, l0_gn_b, l0_w_out, l1_norm, l1_w_in, l1_conv_w, l1_conv_b, l1_dt_bias, l1_a_log, l1_d_skip, l1_gnorm_w, l1_w_out, l2_norm, l2_w_in, l2_q_norm, l2_k_norm, l2_w_out, l3_norm, l3_mu, l3_w_in, l3_w_up, l3_w0, l3_a_up, l3_a0, l3_k_k, l3_k_a, l3_r_k, l3_gn_w, l3_gn_b, l3_w_out, l3_v_up, l3_v0):
    p0 = (l0_mu, l0_w_in, l0_w_up, l0_w0, l0_a_up, l0_a0, l0_k_k, l0_k_a, l0_r_k,
          l0_gn_w, l0_gn_b, l0_w_out)
    p3 = (l3_mu, l3_w_in, l3_w_up, l3_w0, l3_a_up, l3_a0, l3_k_k, l3_k_a, l3_r_k,
          l3_gn_w, l3_gn_b, l3_w_out, l3_v_up, l3_v0)
    p1 = (l1_w_in, l1_conv_w, l1_conv_b, l1_dt_bias, l1_a_log, l1_d_skip, l1_gnorm_w, l1_w_out)
    p2 = (l2_w_in, l2_q_norm, l2_k_norm, l2_w_out)
    x, v_first = _rwkv_layer(x, l0_norm, p0, None)
    x = _mamba_layer(x, l1_norm, p1)
    x = _sb_layer(x, l2_norm, p2)
    x, _ = _rwkv_layer(x, l3_norm, p3, v_first)
    return x
```

```python
import functools
import math

import jax
import jax.numpy as jnp
from jax import lax
from jax.experimental import pallas as pl
from jax.experimental.pallas import tpu as pltpu

F32 = jnp.float32
BF16 = jnp.bfloat16

LANES = 128
HEAD = 64
NORM_EPS = 1e-6
RW_GN_EPS = 64e-5
PROJ_ROWS = 512
OUT_ROWS = 1024
RW_CHUNK = 64
RW_TILES = 8
RW_BATCH = 4
RW_TIME_BLOCK = 128
SSD_CHUNK = 128
SSD_STEP_CHUNKS = 4
SSD_STATE = 128
SSD_GROUPS = 8
SSD_CONV = 4
SB_BLOCK = 256
SB_ROWS = 128
SB_JOINT = 2
SB_DEAD = 104.0
VMEM_LIMIT = 56 * 1024 * 1024


def _cparams(*sem):
    return pltpu.CompilerParams(dimension_semantics=sem, vmem_limit_bytes=VMEM_LIMIT)


def _dot(a, b):
    return jnp.dot(a.astype(BF16), b.astype(BF16), preferred_element_type=F32)


def _dot_nt(a, b):
    return lax.dot_general(a.astype(BF16), b.astype(BF16), (((1,), (1,)), ((), ())),
                           preferred_element_type=F32)


def _dot_tn(a, b):
    return lax.dot_general(a.astype(BF16), b.astype(BF16), (((0,), (0,)), ((), ())),
                           preferred_element_type=F32)


def _split(x, terms):
    out = []
    rem = x
    for i in range(terms):
        p = rem.astype(BF16)
        out.append(p)
        if i + 1 < terms:
            rem = rem - p.astype(F32)
    return out


def _dot_sel_l(sel_bf16, x, terms=3):
    acc = None
    for p in _split(x, terms):
        d = jnp.dot(sel_bf16, p, preferred_element_type=F32)
        acc = d if acc is None else acc + d
    return acc


def _dot_sel_r(x, sel_bf16, terms=3):
    acc = None
    for p in _split(x, terms):
        d = jnp.dot(p, sel_bf16, preferred_element_type=F32)
        acc = d if acc is None else acc + d
    return acc


def _rms(x, g):
    return x * lax.rsqrt(jnp.mean(x * x, axis=-1, keepdims=True) + NORM_EPS) * g


def _sigmoid(x):
    return 1.0 / (1.0 + jnp.exp(-x))


def _silu(x):
    return x * _sigmoid(x)


def _row_tile(t, target):
    tm = min(t, target)
    while t % tm:
        tm //= 2
    return tm


def _const_spec(shape):
    nd = len(shape)
    return pl.BlockSpec(shape, lambda *_: (0,) * nd, pipeline_mode=pl.Buffered(1))


def _rwkv_in_kernel(tiles_per_seq, has_vres, x_ref, xp_ref, g_ref, mu_ref, w4_ref, ws_ref,
                    wup_ref, w0_ref, aup_ref, a0_ref, *rest):
    if has_vres:
        wv_ref, vup_ref, v0_ref, vf_ref, r_o, w_o, k_o, v_o, a_o, g_o = rest
    else:
        r_o, w_o, k_o, v_o, a_o, g_o = rest
    width = r_o.shape[-1]
    g = g_ref[...]
    h = _rms(x_ref[...], g)
    hp = _rms(xp_ref[...], g)[7:8, :]
    first = (pl.program_id(0) % tiles_per_seq) == 0
    hp = jnp.where(first, 0.0, hp)
    rows = lax.broadcasted_iota(jnp.int32, h.shape, 0)
    h_prev = jnp.where(rows == 0, hp, pltpu.roll(h, 1, 0))
    dx = h_prev - h
    mu = mu_ref[...]

    def mix(i):
        return (h + dx * mu[i:i + 1, :]).astype(BF16)

    g_o[...] = _silu(jnp.dot(mix(5), w4_ref[:, 3 * width:4 * width], preferred_element_type=F32))
    w_lo = jnp.tanh(jnp.dot(mix(1), ws_ref[...], preferred_element_type=F32))
    a_lo = jnp.dot(mix(4), ws_ref[...], preferred_element_type=F32)
    w_raw = w0_ref[...] + _dot(w_lo, wup_ref[...])
    w_o[...] = -_sigmoid(w_raw) * math.exp(-0.5)
    a_o[...] = _sigmoid(a0_ref[...] + _dot(a_lo, aup_ref[...]))
    x_v = mix(3)
    v = jnp.dot(x_v, w4_ref[:, 2 * width:3 * width], preferred_element_type=F32)
    if has_vres:
        v_lo = jnp.dot(x_v, wv_ref[...], preferred_element_type=F32)
        gate = _sigmoid(v0_ref[...] + _dot(v_lo, vup_ref[...]))
        v = v + (vf_ref[...] - v) * gate
    v_o[...] = v
    r_o[...] = jnp.dot(mix(0), w4_ref[:, 0:width], preferred_element_type=F32)
    k_o[...] = jnp.dot(mix(2), w4_ref[:, width:2 * width], preferred_element_type=F32)


def _rwkv_in(x2d, seq, norm_g, mu, w_in, w_up, w0, a_up, a0, v_up=None, v0=None, v_first=None):
    n, d = x2d.shape
    width = w_up.shape[1]
    rank_w, rank_a = w_up.shape[0], a_up.shape[0]
    has_vres = v_up is not None
    tm = _row_tile(seq, PROJ_ROWS)
    tiles_per_seq = seq // tm
    w4 = w_in[:, :4 * width].astype(BF16)
    ws = w_in[:, 4 * width:4 * width + rank_w + rank_a].astype(BF16)
    wup = jnp.concatenate([w_up, jnp.zeros((rank_a, width), F32)], 0).astype(BF16)
    aup = jnp.concatenate([jnp.zeros((rank_w, width), F32), a_up], 0).astype(BF16)
    mu8 = jnp.concatenate([mu, jnp.zeros((2, d), F32)], 0)
    row = lambda p: p.reshape(1, -1)
    tile = pl.BlockSpec((tm, d), lambda i: (i, 0))
    wtile = pl.BlockSpec((tm, width), lambda i: (i, 0))
    prev = pl.BlockSpec((8, d), lambda i: (jnp.maximum(i * (tm // 8) - 1, 0), 0))
    args = [x2d, x2d, row(norm_g), mu8, w4, ws, wup, row(w0), aup, row(a0)]
    specs = [tile, prev, _const_spec((1, d)), _const_spec((8, d)), _const_spec(w4.shape),
             _const_spec(ws.shape), _const_spec(wup.shape), _const_spec((1, width)),
             _const_spec(aup.shape), _const_spec((1, width))]
    if has_vres:
        rank_v = v_up.shape[0]
        wv = jnp.pad(w_in[:, 4 * width + rank_w + rank_a:], ((0, 0), (0, LANES - rank_v))).astype(BF16)
        vup = jnp.pad(v_up, ((0, LANES - rank_v), (0, 0))).astype(BF16)
        args += [wv, vup, row(v0), v_first]
        specs += [_const_spec(wv.shape), _const_spec(vup.shape), _const_spec((1, width)), wtile]
    out_shape = [jax.ShapeDtypeStruct((n, width), F32)] * 6
    return pl.pallas_call(
        functools.partial(_rwkv_in_kernel, tiles_per_seq, has_vres),
        grid=(n // tm,),
        in_specs=specs,
        out_specs=[wtile] * 6,
        out_shape=out_shape,
        compiler_params=_cparams("parallel"),
        name="rwkv_in",
    )(*args)


def _rwkv_scan_kernel(r_ref, w_ref, k_ref, v_ref, a_ref, g_ref, kk_p, ka_p, rk_p, gw_p, gb_p,
                      y_ref, st_ref):
    L = RW_CHUNK
    tb = r_ref.shape[1]
    tiles = r_ref.shape[2] // LANES
    n2 = 2 * L
    rows = lax.broadcasted_iota(jnp.int32, (n2, LANES), 0)
    cols = lax.broadcasted_iota(jnp.int32, (n2, LANES), 1)
    own = (rows < L) == (cols < HEAD)
    lower = rows > cols
    lower_eq = rows >= cols
    eye = (rows == cols).astype(F32)
    ri = lax.broadcasted_iota(jnp.int32, (L, L), 0)
    ci = lax.broadcasted_iota(jnp.int32, (L, L), 1)
    tril_incl = (ri >= ci).astype(BF16)

    def dup(x):
        return jnp.concatenate([x, x], axis=0)

    def stack(x):
        return jnp.where(own, dup(x), 0.0)

    @pl.when(pl.program_id(2) == 0)
    def _():
        st_ref[...] = jnp.zeros_like(st_ref)

    def one_tile(bi, p, sl, cw, lw):
        ls = slice(p * LANES, (p + 1) * LANES)
        k_k, k_a, r_k = kk_p[:, ls], ka_p[:, ls], rk_p[:, ls]
        r = r_ref[bi, sl, ls]
        k = k_ref[bi, sl, ls]
        v = v_ref[bi, sl, ls]
        a = a_ref[bi, sl, ls]
        cw_last = cw[L - 1:L, :]
        e_in = jnp.exp(cw)
        e_ex = jnp.exp(cw - lw)
        e_inv = jnp.exp(-cw)
        e_end = jnp.exp(cw_last - cw)
        kk_bd = stack(k * k_k)
        nrm = jnp.sqrt(jnp.sum(kk_bd * kk_bd, axis=-1, keepdims=True))
        kk_bd = kk_bd * (1.0 / jnp.maximum(nrm, 1e-12))
        k_mod = k * (1.0 + (a - 1.0) * k_a)
        at_bd = (-kk_bd * dup(e_ex)).astype(BF16)
        bt_bd = (kk_bd * dup(a * e_inv)).astype(BF16)
        kt_bd = stack(k_mod * e_inv).astype(BF16)
        rt_bd = stack(r * e_in).astype(BF16)
        v_bd = stack(v)
        v_bf = v_bd.astype(BF16)
        bw_bd = (kk_bd * dup(a * e_end)).astype(BF16)
        kw_bd = stack(k_mod * e_end).astype(BF16)

        m = _dot_nt(jnp.concatenate([at_bd, rt_bd], axis=0),
                    jnp.concatenate([bt_bd, kt_bd], axis=0))
        yield
        a_ab = jnp.where(lower, m[:n2, :n2], 0.0)
        a_ak = jnp.where(lower, m[:n2, n2:], 0.0).astype(BF16)
        a_r = jnp.where(jnp.concatenate([lower_eq, lower_eq], axis=1), m[n2:, :], 0.0).astype(BF16)

        x = eye + a_ab
        pw = a_ab.astype(BF16)
        pw = _dot(pw, pw)
        yield
        pw = pw.astype(BF16)
        steps = int(math.log2(L)) - 1
        for s in range(steps):
            if s + 1 < steps:
                px = _dot(pw, jnp.concatenate([pw, x.astype(BF16)], axis=1))
                yield
                pw = px[:, :n2].astype(BF16)
                x = x + px[:, n2:]
            else:
                px = _dot(pw, x)
                yield
                x = x + px

        st = st_ref[bi * tiles + p]
        rhs = _dot_nt(at_bd, st) + _dot(a_ak, v_bf)
        y0 = _dot_nt(rt_bd, st)
        yield
        u = _dot(x, rhs)
        yield
        uv = jnp.concatenate([u.astype(BF16), v_bf], axis=0)
        y = y0 + _dot(a_r, uv)
        st_ref[bi * tiles + p] = st * jnp.exp(cw_last) + _dot_tn(uv, jnp.concatenate([bw_bd, kw_bd], axis=0))
        yield

        gw_bd = stack(jnp.broadcast_to(gw_p[:, ls], (L, LANES)))
        gb_bd = stack(jnp.broadcast_to(gb_p[:, ls], (L, LANES)))
        mean = jnp.sum(y, axis=-1, keepdims=True) * (1.0 / HEAD)
        d = jnp.where(own, y - mean, 0.0)
        var = jnp.sum(d * d, axis=-1, keepdims=True) * (1.0 / HEAD)
        yn = d * lax.rsqrt(var + RW_GN_EPS) * gw_bd + gb_bd
        bonus = jnp.sum(stack(r * k_mod * r_k), axis=-1, keepdims=True)
        yo = yn + bonus * v_bd
        y_ref[bi, sl, ls] = ((yo[:L] + yo[L:]) * g_ref[bi, sl, ls]).astype(BF16)

    def chunk(c, carry):
        sl = pl.ds(pl.multiple_of(c * L, L), L)
        running = []
        for bi in range(r_ref.shape[0]):
            lw = w_ref[bi, sl, :]
            cw = _dot_sel_l(tril_incl, lw)
            running += [one_tile(bi, p, sl, cw[:, p * LANES:(p + 1) * LANES],
                                 lw[:, p * LANES:(p + 1) * LANES]) for p in range(tiles)]
        while running:
            running = [t for t in running if next(t, "done") != "done"]
        return carry

    lax.fori_loop(0, tb // L, chunk, 0)


def _rwkv_scan(r, w, k, v, a, g, k_k, k_a, r_k, gn_w, gn_b):
    b, seq, width = r.shape
    gl = RW_TILES * LANES
    tb = _row_tile(seq, RW_TIME_BLOCK)
    nb = RW_BATCH if b % RW_BATCH == 0 else 1
    blk = pl.BlockSpec((nb, tb, gl), lambda i, j, t: (i, t, j))
    par = pl.BlockSpec((1, gl), lambda i, j, t: (0, j))
    row = lambda p: p.reshape(1, -1)
    return pl.pallas_call(
        _rwkv_scan_kernel,
        grid=(b // nb, width // gl, seq // tb),
        in_specs=[blk] * 6 + [par] * 5,
        out_specs=blk,
        out_shape=jax.ShapeDtypeStruct((b, seq, width), BF16),
        scratch_shapes=[pltpu.VMEM((nb * RW_TILES, LANES, LANES), F32)],
        compiler_params=_cparams("parallel", "parallel", "arbitrary"),
        name="rwkv_scan",
    )(r, w, k, v, a, g, row(k_k), row(k_a), row(r_k), row(gn_w), row(gn_b))


def _out_proj_kernel(x_ref, yg_ref, w_ref, o_ref):
    o_ref[...] = x_ref[...] + jnp.dot(yg_ref[...], w_ref[...], preferred_element_type=F32)


def _out_proj(x2d, yg2d, w_out, seq):
    n, d = x2d.shape
    width = yg2d.shape[1]
    tm = _row_tile(seq, OUT_ROWS)
    return pl.pallas_call(
        _out_proj_kernel,
        grid=(n // tm,),
        in_specs=[pl.BlockSpec((tm, d), lambda i: (i, 0)),
                  pl.BlockSpec((tm, width), lambda i: (i, 0)),
                  _const_spec((width, d))],
        out_specs=pl.BlockSpec((tm, d), lambda i: (i, 0)),
        out_shape=jax.ShapeDtypeStruct((n, d), F32),
        compiler_params=_cparams("parallel"),
        name="out_proj",
    )(x2d, yg2d, w_out.astype(BF16))


def _rwkv_layer(x, norm_g, p, v_first):
    mu, w_in, w_up, w0, a_up, a0, k_k, k_a, r_k, gn_w, gn_b, w_out = p[:12]
    b, seq, d = x.shape
    x2d = x.reshape(b * seq, d)
    if v_first is None:
        r, w, k, v, a, g = _rwkv_in(x2d, seq, norm_g, mu, w_in, w_up, w0, a_up, a0)
        v_first = v
    else:
        v_up, v0 = p[12:]
        r, w, k, v, a, g = _rwkv_in(x2d, seq, norm_g, mu, w_in, w_up, w0, a_up, a0,
                                    v_up, v0, v_first)
    width = r.shape[1]
    sh = lambda t: t.reshape(b, seq, width)
    yg = _rwkv_scan(sh(r), sh(w), sh(k), sh(v), sh(a), sh(g), k_k, k_a, r_k.reshape(-1),
                    gn_w, gn_b)
    out = _out_proj(x2d, yg.reshape(b * seq, width), w_out, seq)
    return out.reshape(b, seq, d), v_first


def _norm_proj_kernel(acts, x_ref, g_ref, *refs):
    n = len(acts)
    w_refs, o_refs = refs[:n], refs[n:]
    h = _rms(x_ref[...], g_ref[...]).astype(BF16)
    for i in sorted(range(n), key=lambda i: not acts[i]):
        y = jnp.dot(h, w_refs[i][...], preferred_element_type=F32)
        o_refs[i][...] = _silu(y) if acts[i] else y


def _norm_proj(x2d, seq, norm_g, weights, acts):
    n, d = x2d.shape
    tm = _row_tile(seq, PROJ_ROWS)
    weights = [w.astype(BF16) for w in weights]
    return pl.pallas_call(
        functools.partial(_norm_proj_kernel, tuple(acts)),
        grid=(n // tm,),
        in_specs=[pl.BlockSpec((tm, d), lambda i: (i, 0)), _const_spec((1, d))]
                 + [_const_spec(w.shape) for w in weights],
        out_specs=[pl.BlockSpec((tm, w.shape[1]), lambda i: (i, 0)) for w in weights],
        out_shape=[jax.ShapeDtypeStruct((n, w.shape[1]), F32) for w in weights],
        compiler_params=_cparams("parallel"),
        name="norm_proj",
    )(x2d, norm_g.reshape(1, d), *weights)


def _sb_in_kernel(x_ref, g_ref, wq_ref, wk_ref, wv_ref, wg_ref, qn_ref, kn_ref,
                  q_o, k_o, v_o, g_o):
    h = _rms(x_ref[...], g_ref[...]).astype(BF16)
    tm, width = q_o.shape
    head0 = lax.broadcasted_iota(jnp.int32, (tm, LANES), 1) < HEAD

    def head_rms(y, w_row, scale, o_ref):
        for j in range(width // LANES):
            ls = slice(j * LANES, (j + 1) * LANES)
            t = y[:, ls]
            sq = t * t
            s0 = jnp.sum(jnp.where(head0, sq, 0.0), axis=-1, keepdims=True)
            s1 = jnp.sum(jnp.where(head0, 0.0, sq), axis=-1, keepdims=True)
            inv = jnp.where(head0, lax.rsqrt(s0 * (1.0 / HEAD) + NORM_EPS),
                            lax.rsqrt(s1 * (1.0 / HEAD) + NORM_EPS))
            o_ref[:, ls] = (t * inv * (w_row[:, ls] * scale)).astype(BF16)

    g_o[...] = _silu(jnp.dot(h, wg_ref[...], preferred_element_type=F32))
    head_rms(jnp.dot(h, wq_ref[...], preferred_element_type=F32), qn_ref[...],
             1.0 / math.sqrt(HEAD), q_o)
    head_rms(jnp.dot(h, wk_ref[...], preferred_element_type=F32), kn_ref[...], 1.0, k_o)
    v_o[...] = jnp.dot(h, wv_ref[...], preferred_element_type=F32).astype(BF16)


def _sb_in(x2d, seq, norm_g, w_in, q_norm, k_norm, width):
    n, d = x2d.shape
    tm = _row_tile(seq, PROJ_ROWS)
    ws = [w_in[:, i * width:(i + 1) * width].astype(BF16) for i in range(4)]
    tile_w = lambda t: jnp.tile(t, width // HEAD).reshape(1, width)
    out_spec = pl.BlockSpec((tm, width), lambda i: (i, 0))
    return pl.pallas_call(
        _sb_in_kernel,
        grid=(n // tm,),
        in_specs=[pl.BlockSpec((tm, d), lambda i: (i, 0)), _const_spec((1, d))]
                 + [_const_spec((d, width))] * 4 + [_const_spec((1, width))] * 2,
        out_specs=[out_spec] * 4,
        out_shape=[jax.ShapeDtypeStruct((n, width), BF16)] * 3
                  + [jax.ShapeDtypeStruct((n, width), F32)],
        compiler_params=_cparams("parallel"),
        name="sb_in",
    )(x2d, norm_g.reshape(1, d), *ws, tile_w(q_norm), tile_w(k_norm))


def _ssd_kernel(x_ref, z_ref, u_ref, dt_ref, *rest):
    *consts, o_ref, ext_ref, h_ref, y_ref = rest
    L = SSD_CHUNK

    def chunk(c, carry):
        rows = pl.ds(pl.multiple_of(c * L, L), L)
        start = jnp.logical_and(pl.program_id(1) == 0, c == 0)
        _ssd_chunk(start, x_ref.at[:, rows], z_ref.at[:, rows], u_ref.at[:, rows],
                   dt_ref.at[:, rows], *consts, o_ref.at[:, rows], ext_ref, h_ref, y_ref)
        return carry
    lax.fori_loop(0, x_ref.shape[1] // L, chunk, 0)


def _ssd_chunk(start, x_ref, z_ref, u_ref, dt_ref, cw_ref, cb_ref, dtb_ref, alog_ref, dsk_ref,
               gnw_ref, exp_ref, wout_ref, o_ref, ext_ref, h_ref, y_ref):
    L = SSD_CHUNK
    width = z_ref.shape[-1]
    heads_per_group = width // HEAD // SSD_GROUPS
    gw = heads_per_group * HEAD

    @pl.when(start)
    def _():
        ext_ref[...] = jnp.zeros_like(ext_ref)
        h_ref[...] = jnp.zeros_like(h_ref)

    u = u_ref[0]
    cdim = u.shape[1]
    u3 = u.reshape(L // 8, 8, cdim)
    tail = ext_ref[...]
    sub = lax.broadcasted_iota(jnp.int32, (L // 8, 8, cdim), 1)
    conv = cb_ref[...] + cw_ref[SSD_CONV - 1:SSD_CONV, :] * u
    for j in range(SSD_CONV - 1):
        s = SSD_CONV - 1 - j
        rot = pltpu.roll(u3, s, 1)
        before = jnp.concatenate([pltpu.roll(tail, s, 0)[None], rot[:-1]], axis=0)
        shifted = jnp.where(sub < s, before, rot).reshape(L, cdim)
        conv = conv + cw_ref[j:j + 1, :] * shifted
    ext_ref[...] = u[L - 8:, :]
    xbc = _silu(conv)
    xs = xbc[:, :width]
    n_bc = SSD_GROUPS * SSD_STATE
    bm = xbc[:, width:width + n_bc]
    cm = xbc[:, width + n_bc:]

    x_in = dt_ref[0] + dtb_ref[...]
    dt = jnp.maximum(x_in, 0.0) + jnp.log1p(jnp.exp(-jnp.abs(x_in)))
    a = dt * (-jnp.exp(alog_ref[...]))
    ri = lax.broadcasted_iota(jnp.int32, (L, L), 0)
    ci = lax.broadcasted_iota(jnp.int32, (L, L), 1)
    causal = ri >= ci
    ac = _dot_sel_l(causal.astype(BF16), a)
    ac_t = ac.T
    expand = exp_ref[...]
    dt_w = _dot_sel_r(dt, expand)
    ac_w = _dot_sel_r(ac, expand)
    ac_last_w = ac_w[L - 1:L, :]
    x_dt = xs * dt_w
    e_out = jnp.exp(ac_w)
    xd = x_dt * jnp.exp(ac_last_w - ac_w)
    e_last = jnp.exp(ac_last_w)
    lane_g = lax.broadcasted_iota(jnp.int32, (L, gw), 1) // HEAD

    def one_group(g):
        bg = bm[:, g * SSD_STATE:(g + 1) * SSD_STATE].astype(BF16)
        cg = cm[:, g * SSD_STATE:(g + 1) * SSD_STATE].astype(BF16)
        gs = slice(g * gw, (g + 1) * gw)
        cb = _dot_nt(cg, bg)
        h_prev = h_ref[g]
        y_off = _dot(cg, h_prev)
        h_new = _dot_tn(bg, xd[:, gs])
        yield
        x_g = x_dt[:, gs].astype(BF16)
        lms, xms = [], []
        for r in range(heads_per_group):
            hd = g * heads_per_group + r
            seg = ac[:, hd:hd + 1] - ac_t[hd:hd + 1, :]
            lms.append((cb * jnp.exp(jnp.where(causal, seg, -1e30))).astype(BF16))
            xms.append(jnp.where(lane_g == r, x_g, jnp.zeros_like(x_g)))
        y_diag = jnp.dot(jnp.concatenate(lms, axis=1), jnp.concatenate(xms, axis=0),
                         preferred_element_type=F32)
        h_ref[g] = h_prev * e_last[:, gs] + h_new
        yield
        y_ref[:, gs] = y_off * e_out[:, gs] + y_diag

    running = [one_group(g) for g in range(SSD_GROUPS)]
    while running:
        running = [t for t in running if next(t, "done") != "done"]

    y = (y_ref[...] + xs * dsk_ref[...]) * _silu(z_ref[0])
    for g in range(SSD_GROUPS):
        gs = slice(g * gw, (g + 1) * gw)
        yg = y[:, gs]
        ms = jnp.mean(yg * yg, axis=-1, keepdims=True)
        y_ref[:, gs] = yg * lax.rsqrt(ms + NORM_EPS)
    yn = (y_ref[...] * gnw_ref[...]).astype(BF16)
    o_ref[0] = x_ref[0] + jnp.dot(yn, wout_ref[...], preferred_element_type=F32)


def _mamba_layer(x, norm_g, p):
    w_in, conv_w, conv_b, dt_bias, a_log, d_skip, gnorm_w, w_out = p
    b, seq, d = x.shape
    width = w_out.shape[0]
    heads = d_skip.shape[0]
    conv_dim = conv_w.shape[1]
    L = SSD_CHUNK
    x2d = x.reshape(b * seq, d)
    w_dt = jnp.pad(w_in[:, width + conv_dim:], ((0, 0), (0, LANES - heads)))
    z, u, dt = _norm_proj(x2d, seq, norm_g,
                          [w_in[:, :width], w_in[:, width:width + conv_dim], w_dt],
                          [False, False, False])
    pad_h = lambda v: jnp.pad(v, (0, LANES - heads)).reshape(1, LANES)
    expand = (jnp.arange(LANES)[:, None] == (jnp.arange(width) // HEAD)[None, :]).astype(BF16)
    cw8 = jnp.pad(conv_w, ((0, 8 - conv_w.shape[0]), (0, 0)))
    gw = width // SSD_GROUPS
    step = _row_tile(seq, SSD_STEP_CHUNKS * L)
    blk = lambda n: pl.BlockSpec((1, step, n), lambda i, j: (i, j, 0))
    out = pl.pallas_call(
        _ssd_kernel,
        grid=(b, seq // step),
        in_specs=[blk(d), blk(width), blk(conv_dim), blk(LANES),
                  _const_spec((8, conv_dim)), _const_spec((1, conv_dim)),
                  _const_spec((1, LANES)), _const_spec((1, LANES)),
                  _const_spec((1, width)), _const_spec((1, width)),
                  _const_spec((LANES, width)), _const_spec((width, d))],
        out_specs=blk(d),
        out_shape=jax.ShapeDtypeStruct((b, seq, d), F32),
        scratch_shapes=[pltpu.VMEM((8, conv_dim), F32),
                        pltpu.VMEM((SSD_GROUPS, SSD_STATE, gw), F32),
                        pltpu.VMEM((L, width), F32)],
        compiler_params=_cparams("parallel", "arbitrary"),
        name="ssd",
    )(x, z.reshape(b, seq, width), u.reshape(b, seq, conv_dim), dt.reshape(b, seq, LANES),
      cw8, conv_b.reshape(1, -1), pad_h(dt_bias), pad_h(a_log),
      jnp.repeat(d_skip, HEAD).reshape(1, -1), gnorm_w.reshape(1, -1), expand,
      w_out.astype(BF16))
    return out


def _sb_kernel(blk, q_ref, ks_ref, vs_ref, g_ref, o_ref):
    n_blocks = q_ref.shape[1] // blk

    def views(i):
        rows = pl.ds(pl.multiple_of(i * blk, blk), blk)
        return i, q_ref.at[:, rows], g_ref.at[:, rows], o_ref.at[:, rows]

    def joint(t, carry):
        _sb_blocks([views(t * SB_JOINT + u) for u in range(SB_JOINT)], ks_ref, vs_ref)
        return carry
    lax.fori_loop(0, n_blocks // SB_JOINT, joint, 0)
    for i in range(n_blocks - n_blocks % SB_JOINT, n_blocks):
        _sb_blocks([views(i)], ks_ref, vs_ref)


def _sb_blocks(blocks, ks_ref, vs_ref):
    blk = blocks[0][1].shape[1]
    lane = lax.broadcasted_iota(jnp.int32, (blk, LANES), 1)
    head0 = lane < HEAD
    ks_ref, vs_ref = ks_ref.at[0], vs_ref.at[0]
    gr = min(SB_ROWS, blk)
    ngroups = 2 * blk // gr
    ji = lax.broadcasted_iota(jnp.int32, (blk, blk), 0)
    si = lax.broadcasted_iota(jnp.int32, (blk, blk), 1)
    suffix = (ji > si).astype(BF16)

    def group(q_bd, g, j, get_run, diagonal, out, present=None):
        first = (g * gr) % blk
        nk = first + gr if diagonal else blk
        sl = pl.ds(pl.multiple_of(j * blk, blk), nk)
        z = _dot_nt(q_bd[g * gr:(g + 1) * gr], ks_ref[sl, :])
        yield
        log_beta = jnp.minimum(z, 0.0) - jnp.log(1.0 + jnp.exp(-jnp.abs(z)))
        log_1m = log_beta - z
        if diagonal:
            strict = (lax.broadcasted_iota(jnp.int32, (gr, nk), 0) + first
                      > lax.broadcasted_iota(jnp.int32, (gr, nk), 1))
            log_1m = jnp.where(strict, log_1m, 0.0)
        tail = _dot_sel_r(log_1m, suffix[:nk, :nk], terms=2)
        total = jnp.sum(log_1m, axis=-1, keepdims=True)
        yield
        run = get_run()
        run_in = run if present is None else jnp.where(present, run, -1e30)
        att = jnp.exp(log_beta + tail + run_in)
        if diagonal:
            att = jnp.where(strict, att, 0.0)
        if present is not None:
            total = jnp.where(present, total, 0.0)
        out[g] = (run + total, _dot(att, vs_ref[sl, :]))
        yield

    def lockstep(gens):
        while gens:
            gens = [t for t in gens if next(t, "done") != "done"]

    def live(runs):
        top = functools.reduce(jnp.maximum, [jnp.max(r) for r in runs])
        return (top > -SB_DEAD).astype(jnp.int32)

    zero = jnp.zeros((gr, 1), F32)
    gens, started = [], []
    for i, q_ref, g_ref, o_ref in blocks:
        q = q_ref[0]
        zq = jnp.zeros_like(q)
        q_bd = jnp.concatenate([jnp.where(head0, q, zq), jnp.where(head0, zq, q)], axis=0)
        d_out, p_out = {}, {}
        gens += [group(q_bd, g, i, lambda: zero, True, d_out) for g in range(ngroups)]
        gens += [group(q_bd, g, jnp.maximum(i - 1, 0), lambda g=g, d_out=d_out: d_out[g][0], False,
                       p_out, i > 0) for g in range(ngroups)]
        started.append((i, q_bd, d_out, p_out, g_ref, o_ref))
    lockstep(gens)

    for i, q_bd, d_out, p_out, g_ref, o_ref in started:
        runs = [p_out[g][0] for g in range(ngroups)]
        accs = [d_out[g][1] + p_out[g][1] for g in range(ngroups)]

        def cond(c, i=i):
            jj, _, _, go = c
            return jnp.logical_and(jj <= i, go > 0)

        def body(c, i=i, q_bd=q_bd):
            jj, runs, accs, _ = c
            out = {}
            lockstep([group(q_bd, g, i - jj, lambda g=g: runs[g], False, out) for g in range(ngroups)])
            runs = [out[g][0] for g in range(ngroups)]
            return jj + 1, runs, [accs[g] + out[g][1] for g in range(ngroups)], live(runs)

        _, _, accs, _ = lax.while_loop(cond, body, (jnp.int32(2), runs, accs, live(runs)))
        acc = jnp.concatenate(accs, axis=0)
        o_ref[0] = (jnp.where(head0, acc[:blk], acc[blk:]) * g_ref[0]).astype(BF16)


def _sb_layer(x, norm_g, p):
    w_in, q_norm, k_norm, w_out = p
    b, seq, d = x.shape
    width = w_out.shape[0]
    x2d = x.reshape(b * seq, d)
    q, k, v, g = _sb_in(x2d, seq, norm_g, w_in, q_norm, k_norm, width)
    sh = lambda t: t.reshape(b, seq, width)
    full = pl.BlockSpec((1, seq, LANES), lambda bi, pi: (bi, 0, pi))
    o = pl.pallas_call(
        functools.partial(_sb_kernel, _row_tile(seq, SB_BLOCK)),
        grid=(b, width // LANES),
        in_specs=[full] * 4,
        out_specs=full,
        out_shape=jax.ShapeDtypeStruct((b, seq, width), BF16),
        compiler_params=_cparams("parallel", "parallel"),
        name="sb_attn",
    )(sh(q), sh(k), sh(v), sh(g))
    out = _out_proj(x2d, o.reshape(b * seq, width), w_out, seq)
    return out.reshape(b, seq, d)


def kernel(x, l0_norm, l0_mu, l0_w_in, l0_w_up, l0_w0, l0_a_up, l0_a0, l0_k_k, l0_k_a, l0_r_k, l0_gn_w, l0_gn_b, l0_w_out, l1_norm, l1_w_in, l1_conv_w, l1_conv_b, l1_dt_bias, l1_a_log, l1_d_skip, l1_gnorm_w, l1_w_out, l2_norm, l2_w_in, l2_q_norm, l2_k_norm, l2_w_out, l3_norm, l3_mu, l3_w_in, l3_w_up, l3_w0, l3_a_up, l3_a0, l3_k_k, l3_k_a, l3_r_k, l3_gn_w, l3_gn_b, l3_w_out, l3_v_up, l3_v0):
    p0 = (l0_mu, l0_w_in, l0_w_up, l0_w0, l0_a_up, l0_a0, l0_k_k, l0_k_a, l0_r_k,
          l0_gn_w, l0_gn_b, l0_w_out)
    p3 = (l3_mu, l3_w_in, l3_w_up, l3_w0, l3_a_up, l3_a0, l3_k_k, l3_k_a, l3_r_k,
          l3_gn_w, l3_gn_b, l3_w_out, l3_v_up, l3_v0)
    p1 = (l1_w_in, l1_conv_w, l1_conv_b, l1_dt_bias, l1_a_log, l1_d_skip, l1_gnorm_w, l1_w_out)
    p2 = (l2_w_in, l2_q_norm, l2_k_norm, l2_w_out)
    x, v_first = _rwkv_layer(x, l0_norm, p0, None)
    x = _mamba_layer(x, l1_norm, p1)
    x = _sb_layer(x, l2_norm, p2)
    x, _ = _rwkv_layer(x, l3_norm, p3, v_first)
    return x
```

```python
import functools
import math

import jax
import jax.numpy as jnp
from jax import lax
from jax.experimental import pallas as pl
from jax.experimental.pallas import tpu as pltpu

F32 = jnp.float32
BF16 = jnp.bfloat16

LANES = 128
HEAD = 64
NORM_EPS = 1e-6
RW_GN_EPS = 64e-5
PROJ_ROWS = 512
OUT_ROWS = 1024
RW_CHUNK = 64
RW_TILES = 8
RW_BATCH = 4
RW_TIME_BLOCK = 128
SSD_CHUNK = 128
SSD_STEP_CHUNKS = 4
SSD_STATE = 128
SSD_GROUPS = 8
SSD_CONV = 4
SB_BLOCK = 256
SB_ROWS = 128
SB_JOINT = 4
SB_DEAD = 104.0
VMEM_LIMIT = 56 * 1024 * 1024


def _cparams(*sem):
    return pltpu.CompilerParams(dimension_semantics=sem, vmem_limit_bytes=VMEM_LIMIT)


def _dot(a, b):
    return jnp.dot(a.astype(BF16), b.astype(BF16), preferred_element_type=F32)


def _dot_nt(a, b):
    return lax.dot_general(a.astype(BF16), b.astype(BF16), (((1,), (1,)), ((), ())),
                           preferred_element_type=F32)


def _dot_tn(a, b):
    return lax.dot_general(a.astype(BF16), b.astype(BF16), (((0,), (0,)), ((), ())),
                           preferred_element_type=F32)


def _split(x, terms):
    out = []
    rem = x
    for i in range(terms):
        p = rem.astype(BF16)
        out.append(p)
        if i + 1 < terms:
            rem = rem - p.astype(F32)
    return out


def _dot_sel_l(sel_bf16, x, terms=3):
    acc = None
    for p in _split(x, terms):
        d = jnp.dot(sel_bf16, p, preferred_element_type=F32)
        acc = d if acc is None else acc + d
    return acc


def _dot_sel_r(x, sel_bf16, terms=3):
    acc = None
    for p in _split(x, terms):
        d = jnp.dot(p, sel_bf16, preferred_element_type=F32)
        acc = d if acc is None else acc + d
    return acc


def _rms(x, g):
    return x * lax.rsqrt(jnp.mean(x * x, axis=-1, keepdims=True) + NORM_EPS) * g


def _sigmoid(x):
    return 1.0 / (1.0 + jnp.exp(-x))


def _silu(x):
    return x * _sigmoid(x)


def _row_tile(t, target):
    tm = min(t, target)
    while t % tm:
        tm //= 2
    return tm


def _const_spec(shape):
    nd = len(shape)
    return pl.BlockSpec(shape, lambda *_: (0,) * nd, pipeline_mode=pl.Buffered(1))


def _rwkv_in_kernel(tiles_per_seq, has_vres, x_ref, xp_ref, g_ref, mu_ref, w4_ref, ws_ref,
                    wup_ref, w0_ref, aup_ref, a0_ref, *rest):
    if has_vres:
        wv_ref, vup_ref, v0_ref, vf_ref, r_o, w_o, k_o, v_o, a_o, g_o = rest
    else:
        r_o, w_o, k_o, v_o, a_o, g_o = rest
    width = r_o.shape[-1]
    g = g_ref[...]
    h = _rms(x_ref[...], g)
    hp = _rms(xp_ref[...], g)[7:8, :]
    first = (pl.program_id(0) % tiles_per_seq) == 0
    hp = jnp.where(first, 0.0, hp)
    rows = lax.broadcasted_iota(jnp.int32, h.shape, 0)
    h_prev = jnp.where(rows == 0, hp, pltpu.roll(h, 1, 0))
    dx = h_prev - h
    mu = mu_ref[...]

    def mix(i):
        return (h + dx * mu[i:i + 1, :]).astype(BF16)

    g_o[...] = _silu(jnp.dot(mix(5), w4_ref[:, 3 * width:4 * width], preferred_element_type=F32))
    w_lo = jnp.tanh(jnp.dot(mix(1), ws_ref[...], preferred_element_type=F32))
    a_lo = jnp.dot(mix(4), ws_ref[...], preferred_element_type=F32)
    w_raw = w0_ref[...] + _dot(w_lo, wup_ref[...])
    w_o[...] = -_sigmoid(w_raw) * math.exp(-0.5)
    a_o[...] = _sigmoid(a0_ref[...] + _dot(a_lo, aup_ref[...]))
    x_v = mix(3)
    v = jnp.dot(x_v, w4_ref[:, 2 * width:3 * width], preferred_element_type=F32)
    if has_vres:
        v_lo = jnp.dot(x_v, wv_ref[...], preferred_element_type=F32)
        gate = _sigmoid(v0_ref[...] + _dot(v_lo, vup_ref[...]))
        v = v + (vf_ref[...] - v) * gate
    v_o[...] = v
    r_o[...] = jnp.dot(mix(0), w4_ref[:, 0:width], preferred_element_type=F32)
    k_o[...] = jnp.dot(mix(2), w4_ref[:, width:2 * width], preferred_element_type=F32)


def _rwkv_in(x2d, seq, norm_g, mu, w_in, w_up, w0, a_up, a0, v_up=None, v0=None, v_first=None):
    n, d = x2d.shape
    width = w_up.shape[1]
    rank_w, rank_a = w_up.shape[0], a_up.shape[0]
    has_vres = v_up is not None
    tm = _row_tile(seq, PROJ_ROWS)
    tiles_per_seq = seq // tm
    w4 = w_in[:, :4 * width].astype(BF16)
    ws = w_in[:, 4 * width:4 * width + rank_w + rank_a].astype(BF16)
    wup = jnp.concatenate([w_up, jnp.zeros((rank_a, width), F32)], 0).astype(BF16)
    aup = jnp.concatenate([jnp.zeros((rank_w, width), F32), a_up], 0).astype(BF16)
    mu8 = jnp.concatenate([mu, jnp.zeros((2, d), F32)], 0)
    row = lambda p: p.reshape(1, -1)
    tile = pl.BlockSpec((tm, d), lambda i: (i, 0))
    wtile = pl.BlockSpec((tm, width), lambda i: (i, 0))
    prev = pl.BlockSpec((8, d), lambda i: (jnp.maximum(i * (tm // 8) - 1, 0), 0))
    args = [x2d, x2d, row(norm_g), mu8, w4, ws, wup, row(w0), aup, row(a0)]
    specs = [tile, prev, _const_spec((1, d)), _const_spec((8, d)), _const_spec(w4.shape),
             _const_spec(ws.shape), _const_spec(wup.shape), _const_spec((1, width)),
             _const_spec(aup.shape), _const_spec((1, width))]
    if has_vres:
        rank_v = v_up.shape[0]
        wv = jnp.pad(w_in[:, 4 * width + rank_w + rank_a:], ((0, 0), (0, LANES - rank_v))).astype(BF16)
        vup = jnp.pad(v_up, ((0, LANES - rank_v), (0, 0))).astype(BF16)
        args += [wv, vup, row(v0), v_first]
        specs += [_const_spec(wv.shape), _const_spec(vup.shape), _const_spec((1, width)), wtile]
    out_shape = [jax.ShapeDtypeStruct((n, width), F32)] * 6
    return pl.pallas_call(
        functools.partial(_rwkv_in_kernel, tiles_per_seq, has_vres),
        grid=(n // tm,),
        in_specs=specs,
        out_specs=[wtile] * 6,
        out_shape=out_shape,
        compiler_params=_cparams("parallel"),
        name="rwkv_in",
    )(*args)


def _rwkv_scan_kernel(r_ref, w_ref, k_ref, v_ref, a_ref, g_ref, kk_p, ka_p, rk_p, gw_p, gb_p,
                      y_ref, st_ref):
    L = RW_CHUNK
    tb = r_ref.shape[1]
    tiles = r_ref.shape[2] // LANES
    n2 = 2 * L
    rows = lax.broadcasted_iota(jnp.int32, (n2, LANES), 0)
    cols = lax.broadcasted_iota(jnp.int32, (n2, LANES), 1)
    own = (rows < L) == (cols < HEAD)
    lower = rows > cols
    lower_eq = rows >= cols
    eye = (rows == cols).astype(F32)
    ri = lax.broadcasted_iota(jnp.int32, (L, L), 0)
    ci = lax.broadcasted_iota(jnp.int32, (L, L), 1)
    tril_incl = (ri >= ci).astype(BF16)

    def dup(x):
        return jnp.concatenate([x, x], axis=0)

    def stack(x):
        return jnp.where(own, dup(x), 0.0)

    @pl.when(pl.program_id(2) == 0)
    def _():
        st_ref[...] = jnp.zeros_like(st_ref)

    def one_tile(bi, p, sl, cw, lw):
        ls = slice(p * LANES, (p + 1) * LANES)
        k_k, k_a, r_k = kk_p[:, ls], ka_p[:, ls], rk_p[:, ls]
        r = r_ref[bi, sl, ls]
        k = k_ref[bi, sl, ls]
        v = v_ref[bi, sl, ls]
        a = a_ref[bi, sl, ls]
        cw_last = cw[L - 1:L, :]
        e_in = jnp.exp(cw)
        e_ex = jnp.exp(cw - lw)
        e_inv = jnp.exp(-cw)
        e_end = jnp.exp(cw_last - cw)
        kk_bd = stack(k * k_k)
        nrm = jnp.sqrt(jnp.sum(kk_bd * kk_bd, axis=-1, keepdims=True))
        kk_bd = kk_bd * (1.0 / jnp.maximum(nrm, 1e-12))
        k_mod = k * (1.0 + (a - 1.0) * k_a)
        at_bd = (-kk_bd * dup(e_ex)).astype(BF16)
        bt_bd = (kk_bd * dup(a * e_inv)).astype(BF16)
        kt_bd = stack(k_mod * e_inv).astype(BF16)
        rt_bd = stack(r * e_in).astype(BF16)
        v_bd = stack(v)
        v_bf = v_bd.astype(BF16)
        bw_bd = (kk_bd * dup(a * e_end)).astype(BF16)
        kw_bd = stack(k_mod * e_end).astype(BF16)

        m = _dot_nt(jnp.concatenate([at_bd, rt_bd], axis=0),
                    jnp.concatenate([bt_bd, kt_bd], axis=0))
        yield
        a_ab = jnp.where(lower, m[:n2, :n2], 0.0)
        a_ak = jnp.where(lower, m[:n2, n2:], 0.0).astype(BF16)
        a_r = jnp.where(jnp.concatenate([lower_eq, lower_eq], axis=1), m[n2:, :], 0.0).astype(BF16)

        x = eye + a_ab
        pw = a_ab.astype(BF16)
        pw = _dot(pw, pw)
        yield
        pw = pw.astype(BF16)
        steps = int(math.log2(L)) - 1
        for s in range(steps):
            if s + 1 < steps:
                px = _dot(pw, jnp.concatenate([pw, x.astype(BF16)], axis=1))
                yield
                pw = px[:, :n2].astype(BF16)
                x = x + px[:, n2:]
            else:
                px = _dot(pw, x)
                yield
                x = x + px

        st = st_ref[bi * tiles + p]
        rhs = _dot_nt(at_bd, st) + _dot(a_ak, v_bf)
        y0 = _dot_nt(rt_bd, st)
        yield
        u = _dot(x, rhs)
        yield
        uv = jnp.concatenate([u.astype(BF16), v_bf], axis=0)
        y = y0 + _dot(a_r, uv)
        st_ref[bi * tiles + p] = st * jnp.exp(cw_last) + _dot_tn(uv, jnp.concatenate([bw_bd, kw_bd], axis=0))
        yield

        gw_bd = stack(jnp.broadcast_to(gw_p[:, ls], (L, LANES)))
        gb_bd = stack(jnp.broadcast_to(gb_p[:, ls], (L, LANES)))
        mean = jnp.sum(y, axis=-1, keepdims=True) * (1.0 / HEAD)
        d = jnp.where(own, y - mean, 0.0)
        var = jnp.sum(d * d, axis=-1, keepdims=True) * (1.0 / HEAD)
        yn = d * lax.rsqrt(var + RW_GN_EPS) * gw_bd + gb_bd
        bonus = jnp.sum(stack(r * k_mod * r_k), axis=-1, keepdims=True)
        yo = yn + bonus * v_bd
        y_ref[bi, sl, ls] = ((yo[:L] + yo[L:]) * g_ref[bi, sl, ls]).astype(BF16)

    def chunk(c, carry):
        sl = pl.ds(pl.multiple_of(c * L, L), L)
        running = []
        for bi in range(r_ref.shape[0]):
            lw = w_ref[bi, sl, :]
            cw = _dot_sel_l(tril_incl, lw)
            running += [one_tile(bi, p, sl, cw[:, p * LANES:(p + 1) * LANES],
                                 lw[:, p * LANES:(p + 1) * LANES]) for p in range(tiles)]
        while running:
            running = [t for t in running if next(t, "done") != "done"]
        return carry

    lax.fori_loop(0, tb // L, chunk, 0)


def _rwkv_scan(r, w, k, v, a, g, k_k, k_a, r_k, gn_w, gn_b):
    b, seq, width = r.shape
    gl = RW_TILES * LANES
    tb = _row_tile(seq, RW_TIME_BLOCK)
    nb = RW_BATCH if b % RW_BATCH == 0 else 1
    blk = pl.BlockSpec((nb, tb, gl), lambda i, j, t: (i, t, j))
    par = pl.BlockSpec((1, gl), lambda i, j, t: (0, j))
    row = lambda p: p.reshape(1, -1)
    return pl.pallas_call(
        _rwkv_scan_kernel,
        grid=(b // nb, width // gl, seq // tb),
        in_specs=[blk] * 6 + [par] * 5,
        out_specs=blk,
        out_shape=jax.ShapeDtypeStruct((b, seq, width), BF16),
        scratch_shapes=[pltpu.VMEM((nb * RW_TILES, LANES, LANES), F32)],
        compiler_params=_cparams("parallel", "parallel", "arbitrary"),
        name="rwkv_scan",
    )(r, w, k, v, a, g, row(k_k), row(k_a), row(r_k), row(gn_w), row(gn_b))


def _out_proj_kernel(x_ref, yg_ref, w_ref, o_ref):
    o_ref[...] = x_ref[...] + jnp.dot(yg_ref[...], w_ref[...], preferred_element_type=F32)


def _out_proj(x2d, yg2d, w_out, seq):
    n, d = x2d.shape
    width = yg2d.shape[1]
    tm = _row_tile(seq, OUT_ROWS)
    return pl.pallas_call(
        _out_proj_kernel,
        grid=(n // tm,),
        in_specs=[pl.BlockSpec((tm, d), lambda i: (i, 0)),
                  pl.BlockSpec((tm, width), lambda i: (i, 0)),
                  _const_spec((width, d))],
        out_specs=pl.BlockSpec((tm, d), lambda i: (i, 0)),
        out_shape=jax.ShapeDtypeStruct((n, d), F32),
        compiler_params=_cparams("parallel"),
        name="out_proj",
    )(x2d, yg2d, w_out.astype(BF16))


def _rwkv_layer(x, norm_g, p, v_first):
    mu, w_in, w_up, w0, a_up, a0, k_k, k_a, r_k, gn_w, gn_b, w_out = p[:12]
    b, seq, d = x.shape
    x2d = x.reshape(b * seq, d)
    if v_first is None:
        r, w, k, v, a, g = _rwkv_in(x2d, seq, norm_g, mu, w_in, w_up, w0, a_up, a0)
        v_first = v
    else:
        v_up, v0 = p[12:]
        r, w, k, v, a, g = _rwkv_in(x2d, seq, norm_g, mu, w_in, w_up, w0, a_up, a0,
                                    v_up, v0, v_first)
    width = r.shape[1]
    sh = lambda t: t.reshape(b, seq, width)
    yg = _rwkv_scan(sh(r), sh(w), sh(k), sh(v), sh(a), sh(g), k_k, k_a, r_k.reshape(-1),
                    gn_w, gn_b)
    out = _out_proj(x2d, yg.reshape(b * seq, width), w_out, seq)
    return out.reshape(b, seq, d), v_first


def _norm_proj_kernel(acts, x_ref, g_ref, *refs):
    n = len(acts)
    w_refs, o_refs = refs[:n], refs[n:]
    h = _rms(x_ref[...], g_ref[...]).astype(BF16)
    for i in sorted(range(n), key=lambda i: not acts[i]):
        y = jnp.dot(h, w_refs[i][...], preferred_element_type=F32)
        o_refs[i][...] = _silu(y) if acts[i] else y


def _norm_proj(x2d, seq, norm_g, weights, acts):
    n, d = x2d.shape
    tm = _row_tile(seq, PROJ_ROWS)
    weights = [w.astype(BF16) for w in weights]
    return pl.pallas_call(
        functools.partial(_norm_proj_kernel, tuple(acts)),
        grid=(n // tm,),
        in_specs=[pl.BlockSpec((tm, d), lambda i: (i, 0)), _const_spec((1, d))]
                 + [_const_spec(w.shape) for w in weights],
        out_specs=[pl.BlockSpec((tm, w.shape[1]), lambda i: (i, 0)) for w in weights],
        out_shape=[jax.ShapeDtypeStruct((n, w.shape[1]), F32) for w in weights],
        compiler_params=_cparams("parallel"),
        name="norm_proj",
    )(x2d, norm_g.reshape(1, d), *weights)


def _sb_in_kernel(x_ref, g_ref, wq_ref, wk_ref, wv_ref, wg_ref, qn_ref, kn_ref,
                  q_o, k_o, v_o, g_o):
    h = _rms(x_ref[...], g_ref[...]).astype(BF16)
    tm, width = q_o.shape
    head0 = lax.broadcasted_iota(jnp.int32, (tm, LANES), 1) < HEAD

    def head_rms(y, w_row, scale, o_ref):
        for j in range(width // LANES):
            ls = slice(j * LANES, (j + 1) * LANES)
            t = y[:, ls]
            sq = t * t
            s0 = jnp.sum(jnp.where(head0, sq, 0.0), axis=-1, keepdims=True)
            s1 = jnp.sum(jnp.where(head0, 0.0, sq), axis=-1, keepdims=True)
            inv = jnp.where(head0, lax.rsqrt(s0 * (1.0 / HEAD) + NORM_EPS),
                            lax.rsqrt(s1 * (1.0 / HEAD) + NORM_EPS))
            o_ref[:, ls] = (t * inv * (w_row[:, ls] * scale)).astype(BF16)

    g_o[...] = _silu(jnp.dot(h, wg_ref[...], preferred_element_type=F32))
    head_rms(jnp.dot(h, wq_ref[...], preferred_element_type=F32), qn_ref[...],
             1.0 / math.sqrt(HEAD), q_o)
    head_rms(jnp.dot(h, wk_ref[...], preferred_element_type=F32), kn_ref[...], 1.0, k_o)
    v_o[...] = jnp.dot(h, wv_ref[...], preferred_element_type=F32).astype(BF16)


def _sb_in(x2d, seq, norm_g, w_in, q_norm, k_norm, width):
    n, d = x2d.shape
    tm = _row_tile(seq, PROJ_ROWS)
    ws = [w_in[:, i * width:(i + 1) * width].astype(BF16) for i in range(4)]
    tile_w = lambda t: jnp.tile(t, width // HEAD).reshape(1, width)
    out_spec = pl.BlockSpec((tm, width), lambda i: (i, 0))
    return pl.pallas_call(
        _sb_in_kernel,
        grid=(n // tm,),
        in_specs=[pl.BlockSpec((tm, d), lambda i: (i, 0)), _const_spec((1, d))]
                 + [_const_spec((d, width))] * 4 + [_const_spec((1, width))] * 2,
        out_specs=[out_spec] * 4,
        out_shape=[jax.ShapeDtypeStruct((n, width), BF16)] * 3
                  + [jax.ShapeDtypeStruct((n, width), F32)],
        compiler_params=_cparams("parallel"),
        name="sb_in",
    )(x2d, norm_g.reshape(1, d), *ws, tile_w(q_norm), tile_w(k_norm))


def _ssd_kernel(x_ref, z_ref, u_ref, dt_ref, *rest):
    *consts, o_ref, ext_ref, h_ref, y_ref = rest
    L = SSD_CHUNK

    def chunk(c, carry):
        rows = pl.ds(pl.multiple_of(c * L, L), L)
        start = jnp.logical_and(pl.program_id(1) == 0, c == 0)
        _ssd_chunk(start, x_ref.at[:, rows], z_ref.at[:, rows], u_ref.at[:, rows],
                   dt_ref.at[:, rows], *consts, o_ref.at[:, rows], ext_ref, h_ref, y_ref)
        return carry
    lax.fori_loop(0, x_ref.shape[1] // L, chunk, 0)


def _ssd_chunk(start, x_ref, z_ref, u_ref, dt_ref, cw_ref, cb_ref, dtb_ref, alog_ref, dsk_ref,
               gnw_ref, exp_ref, wout_ref, o_ref, ext_ref, h_ref, y_ref):
    L = SSD_CHUNK
    width = z_ref.shape[-1]
    heads_per_group = width // HEAD // SSD_GROUPS
    gw = heads_per_group * HEAD

    @pl.when(start)
    def _():
        ext_ref[...] = jnp.zeros_like(ext_ref)
        h_ref[...] = jnp.zeros_like(h_ref)

    u = u_ref[0]
    cdim = u.shape[1]
    u3 = u.reshape(L // 8, 8, cdim)
    tail = ext_ref[...]
    sub = lax.broadcasted_iota(jnp.int32, (L // 8, 8, cdim), 1)
    conv = cb_ref[...] + cw_ref[SSD_CONV - 1:SSD_CONV, :] * u
    for j in range(SSD_CONV - 1):
        s = SSD_CONV - 1 - j
        rot = pltpu.roll(u3, s, 1)
        before = jnp.concatenate([pltpu.roll(tail, s, 0)[None], rot[:-1]], axis=0)
        shifted = jnp.where(sub < s, before, rot).reshape(L, cdim)
        conv = conv + cw_ref[j:j + 1, :] * shifted
    ext_ref[...] = u[L - 8:, :]
    xbc = _silu(conv)
    xs = xbc[:, :width]
    n_bc = SSD_GROUPS * SSD_STATE
    bm = xbc[:, width:width + n_bc]
    cm = xbc[:, width + n_bc:]

    x_in = dt_ref[0] + dtb_ref[...]
    dt = jnp.maximum(x_in, 0.0) + jnp.log1p(jnp.exp(-jnp.abs(x_in)))
    a = dt * (-jnp.exp(alog_ref[...]))
    ri = lax.broadcasted_iota(jnp.int32, (L, L), 0)
    ci = lax.broadcasted_iota(jnp.int32, (L, L), 1)
    causal = ri >= ci
    ac = _dot_sel_l(causal.astype(BF16), a)
    ac_t = ac.T
    expand = exp_ref[...]
    dt_w = _dot_sel_r(dt, expand)
    ac_w = _dot_sel_r(ac, expand)
    ac_last_w = ac_w[L - 1:L, :]
    x_dt = xs * dt_w
    e_out = jnp.exp(ac_w)
    xd = x_dt * jnp.exp(ac_last_w - ac_w)
    e_last = jnp.exp(ac_last_w)
    lane_g = lax.broadcasted_iota(jnp.int32, (L, gw), 1) // HEAD

    def one_group(g):
        bg = bm[:, g * SSD_STATE:(g + 1) * SSD_STATE].astype(BF16)
        cg = cm[:, g * SSD_STATE:(g + 1) * SSD_STATE].astype(BF16)
        gs = slice(g * gw, (g + 1) * gw)
        cb = _dot_nt(cg, bg)
        h_prev = h_ref[g]
        y_off = _dot(cg, h_prev)
        h_new = _dot_tn(bg, xd[:, gs])
        yield
        x_g = x_dt[:, gs].astype(BF16)
        lms, xms = [], []
        for r in range(heads_per_group):
            hd = g * heads_per_group + r
            seg = ac[:, hd:hd + 1] - ac_t[hd:hd + 1, :]
            lms.append((cb * jnp.exp(jnp.where(causal, seg, -1e30))).astype(BF16))
            xms.append(jnp.where(lane_g == r, x_g, jnp.zeros_like(x_g)))
        y_diag = jnp.dot(jnp.concatenate(lms, axis=1), jnp.concatenate(xms, axis=0),
                         preferred_element_type=F32)
        h_ref[g] = h_prev * e_last[:, gs] + h_new
        yield
        y_ref[:, gs] = y_off * e_out[:, gs] + y_diag

    running = [one_group(g) for g in range(SSD_GROUPS)]
    while running:
        running = [t for t in running if next(t, "done") != "done"]

    y = (y_ref[...] + xs * dsk_ref[...]) * _silu(z_ref[0])
    for g in range(SSD_GROUPS):
        gs = slice(g * gw, (g + 1) * gw)
        yg = y[:, gs]
        ms = jnp.mean(yg * yg, axis=-1, keepdims=True)
        y_ref[:, gs] = yg * lax.rsqrt(ms + NORM_EPS)
    yn = (y_ref[...] * gnw_ref[...]).astype(BF16)
    o_ref[0] = x_ref[0] + jnp.dot(yn, wout_ref[...], preferred_element_type=F32)


def _mamba_layer(x, norm_g, p):
    w_in, conv_w, conv_b, dt_bias, a_log, d_skip, gnorm_w, w_out = p
    b, seq, d = x.shape
    width = w_out.shape[0]
    heads = d_skip.shape[0]
    conv_dim = conv_w.shape[1]
    L = SSD_CHUNK
    x2d = x.reshape(b * seq, d)
    w_dt = jnp.pad(w_in[:, width + conv_dim:], ((0, 0), (0, LANES - heads)))
    z, u, dt = _norm_proj(x2d, seq, norm_g,
                          [w_in[:, :width], w_in[:, width:width + conv_dim], w_dt],
                          [False, False, False])
    pad_h = lambda v: jnp.pad(v, (0, LANES - heads)).reshape(1, LANES)
    expand = (jnp.arange(LANES)[:, None] == (jnp.arange(width) // HEAD)[None, :]).astype(BF16)
    cw8 = jnp.pad(conv_w, ((0, 8 - conv_w.shape[0]), (0, 0)))
    gw = width // SSD_GROUPS
    step = _row_tile(seq, SSD_STEP_CHUNKS * L)
    blk = lambda n: pl.BlockSpec((1, step, n), lambda i, j: (i, j, 0))
    out = pl.pallas_call(
        _ssd_kernel,
        grid=(b, seq // step),
        in_specs=[blk(d), blk(width), blk(conv_dim), blk(LANES),
                  _const_spec((8, conv_dim)), _const_spec((1, conv_dim)),
                  _const_spec((1, LANES)), _const_spec((1, LANES)),
                  _const_spec((1, width)), _const_spec((1, width)),
                  _const_spec((LANES, width)), _const_spec((width, d))],
        out_specs=blk(d),
        out_shape=jax.ShapeDtypeStruct((b, seq, d), F32),
        scratch_shapes=[pltpu.VMEM((8, conv_dim), F32),
                        pltpu.VMEM((SSD_GROUPS, SSD_STATE, gw), F32),
                        pltpu.VMEM((L, width), F32)],
        compiler_params=_cparams("parallel", "arbitrary"),
        name="ssd",
    )(x, z.reshape(b, seq, width), u.reshape(b, seq, conv_dim), dt.reshape(b, seq, LANES),
      cw8, conv_b.reshape(1, -1), pad_h(dt_bias), pad_h(a_log),
      jnp.repeat(d_skip, HEAD).reshape(1, -1), gnorm_w.reshape(1, -1), expand,
      w_out.astype(BF16))
    return out


def _sb_kernel(blk, q_ref, ks_ref, vs_ref, g_ref, o_ref):
    n_blocks = q_ref.shape[1] // blk

    def views(i):
        rows = pl.ds(pl.multiple_of(i * blk, blk), blk)
        return i, q_ref.at[:, rows], g_ref.at[:, rows], o_ref.at[:, rows]

    def joint(t, carry):
        _sb_blocks([views(t * SB_JOINT + u) for u in range(SB_JOINT)], ks_ref, vs_ref)
        return carry
    lax.fori_loop(0, n_blocks // SB_JOINT, joint, 0)
    for i in range(n_blocks - n_blocks % SB_JOINT, n_blocks):
        _sb_blocks([views(i)], ks_ref, vs_ref)


def _sb_blocks(blocks, ks_ref, vs_ref):
    blk = blocks[0][1].shape[1]
    lane = lax.broadcasted_iota(jnp.int32, (blk, LANES), 1)
    head0 = lane < HEAD
    ks_ref, vs_ref = ks_ref.at[0], vs_ref.at[0]
    gr = min(SB_ROWS, blk)
    ngroups = 2 * blk // gr
    ji = lax.broadcasted_iota(jnp.int32, (blk, blk), 0)
    si = lax.broadcasted_iota(jnp.int32, (blk, blk), 1)
    suffix = (ji > si).astype(BF16)

    def group(q_bd, g, j, get_run, diagonal, out, present=None):
        first = (g * gr) % blk
        nk = first + gr if diagonal else blk
        sl = pl.ds(pl.multiple_of(j * blk, blk), nk)
        z = _dot_nt(q_bd[g * gr:(g + 1) * gr], ks_ref[sl, :])
        yield
        log_beta = jnp.minimum(z, 0.0) - jnp.log(1.0 + jnp.exp(-jnp.abs(z)))
        log_1m = log_beta - z
        if diagonal:
            strict = (lax.broadcasted_iota(jnp.int32, (gr, nk), 0) + first
                      > lax.broadcasted_iota(jnp.int32, (gr, nk), 1))
            log_1m = jnp.where(strict, log_1m, 0.0)
        tail = _dot_sel_r(log_1m, suffix[:nk, :nk], terms=2)
        total = jnp.sum(log_1m, axis=-1, keepdims=True)
        yield
        run = get_run()
        run_in = run if present is None else jnp.where(present, run, -1e30)
        att = jnp.exp(log_beta + tail + run_in)
        if diagonal:
            att = jnp.where(strict, att, 0.0)
        if present is not None:
            total = jnp.where(present, total, 0.0)
        out[g] = (run + total, _dot(att, vs_ref[sl, :]))
        yield

    def lockstep(gens):
        while gens:
            gens = [t for t in gens if next(t, "done") != "done"]

    def live(runs):
        top = functools.reduce(jnp.maximum, [jnp.max(r) for r in runs])
        return (top > -SB_DEAD).astype(jnp.int32)

    zero = jnp.zeros((gr, 1), F32)
    gens, started = [], []
    for i, q_ref, g_ref, o_ref in blocks:
        q = q_ref[0]
        zq = jnp.zeros_like(q)
        q_bd = jnp.concatenate([jnp.where(head0, q, zq), jnp.where(head0, zq, q)], axis=0)
        d_out, p_out = {}, {}
        gens += [group(q_bd, g, i, lambda: zero, True, d_out) for g in range(ngroups)]
        gens += [group(q_bd, g, jnp.maximum(i - 1, 0), lambda g=g, d_out=d_out: d_out[g][0], False,
                       p_out, i > 0) for g in range(ngroups)]
        started.append((i, q_bd, d_out, p_out, g_ref, o_ref))
    lockstep(gens)

    for i, q_bd, d_out, p_out, g_ref, o_ref in started:
        runs = [p_out[g][0] for g in range(ngroups)]
        accs = [d_out[g][1] + p_out[g][1] for g in range(ngroups)]

        def cond(c, i=i):
            jj, _, _, go = c
            return jnp.logical_and(jj <= i, go > 0)

        def body(c, i=i, q_bd=q_bd):
            jj, runs, accs, _ = c
            out = {}
            lockstep([group(q_bd, g, i - jj, lambda g=g: runs[g], False, out) for g in range(ngroups)])
            runs = [out[g][0] for g in range(ngroups)]
            return jj + 1, runs, [accs[g] + out[g][1] for g in range(ngroups)], live(runs)

        _, _, accs, _ = lax.while_loop(cond, body, (jnp.int32(2), runs, accs, live(runs)))
        acc = jnp.concatenate(accs, axis=0)
        o_ref[0] = (jnp.where(head0, acc[:blk], acc[blk:]) * g_ref[0]).astype(BF16)


def _sb_layer(x, norm_g, p):
    w_in, q_norm, k_norm, w_out = p
    b, seq, d = x.shape
    width = w_out.shape[0]
    x2d = x.reshape(b * seq, d)
    q, k, v, g = _sb_in(x2d, seq, norm_g, w_in, q_norm, k_norm, width)
    sh = lambda t: t.reshape(b, seq, width)
    full = pl.BlockSpec((1, seq, LANES), lambda bi, pi: (bi, 0, pi))
    o = pl.pallas_call(
        functools.partial(_sb_kernel, _row_tile(seq, SB_BLOCK)),
        grid=(b, width // LANES),
        in_specs=[full] * 4,
        out_specs=full,
        out_shape=jax.ShapeDtypeStruct((b, seq, width), BF16),
        compiler_params=_cparams("parallel", "parallel"),
        name="sb_attn",
    )(sh(q), sh(k), sh(v), sh(g))
    out = _out_proj(x2d, o.reshape(b * seq, width), w_out, seq)
    return out.reshape(b, seq, d)


def kernel(x, l0_norm, l0_mu, l0_w_in, l0_w_up, l0_w0, l0_a_up, l0_a0, l0_k_k, l0_k_a, l0_r_k, l0_gn_w, l0_gn_b, l0_w_out, l1_norm, l1_w_in, l1_conv_w, l1_conv_b, l1_dt_bias, l1_a_log, l1_d_skip, l1_gnorm_w, l1_w_out, l2_norm, l2_w_in, l2_q_norm, l2_k_norm, l2_w_out, l3_norm, l3_mu, l3_w_in, l3_w_up, l3_w0, l3_a_up, l3_a0, l3_k_k, l3_k_a, l3_r_k, l3_gn_w, l3_gn_b, l3_w_out, l3_v_up, l3_v0):
    p0 = (l0_mu, l0_w_in, l0_w_up, l0_w0, l0_a_up, l0_a0, l0_k_k, l0_k_a, l0_r_k,
          l0_gn_w, l0_gn_b, l0_w_out)
    p3 = (l3_mu, l3_w_in, l3_w_up, l3_w0, l3_a_up, l3_a0, l3_k_k, l3_k_a, l3_r_k,
          l3_gn_w, l3_gn_b, l3_w_out, l3_v_up, l3_v0)
    p1 = (l1_w_in, l1_conv_w, l1_conv_b, l1_dt_bias, l1_a_log, l1_d_skip, l1_gnorm_w, l1_w_out)
    p2 = (l2_w_in, l2_q_norm, l2_k_norm, l2_w_out)
    x, v_first = _rwkv_layer(x, l0_norm, p0, None)
    x = _mamba_layer(x, l1_norm, p1)
    x = _sb_layer(x, l2_norm, p2)
    x, _ = _rwkv_layer(x, l3_norm, p3, v_first)
    return x
```
